```python
import math
import jax, jax.numpy as jnp
from jax import lax
import numpy as np

D_MODEL = 2048
BATCH = 16
SEQ = 2048
DEPTH = 1

CHUNK = 64
D_MIX = D_MODEL
D_REC = D_MIX // 2
REC_BLOCKS = 8
REC_BLOCK_W = D_REC // REC_BLOCKS
CONV_W = 4
LRU_C = 8.0
D_ATT = D_MIX - D_REC
HEAD_DIM = 128
N_HEADS = D_ATT // HEAD_DIM
Q_BLOCK = 128
D_FF = -(-8 * D_MODEL // (3 * 256)) * 256
D_IN = 2 * D_REC + 3 * D_ATT + N_HEADS
EPS = 1e-6

kernel_name = "hymba_rglru_fox_adaln_block"


def rms_norm(x, g):
    xf = x.astype(jnp.float32)
    y = xf * lax.rsqrt(jnp.mean(xf * xf, axis=-1, keepdims=True) + EPS)
    return (y * g.astype(jnp.float32)).astype(x.dtype)


def modulate(h, shift, scale):
    return h * (1.0 + scale[:, None, :]) + shift[:, None, :]


def causal_depthwise_conv(u, w, b):
    S = u.shape[1]
    up = jnp.pad(u, ((0, 0), (CONV_W - 1, 0), (0, 0)))
    y = up[:, 0:S] * w[0]
    for k in range(1, CONV_W):
        y = y + up[:, k:k + S] * w[k]
    return y + b


def block_diag_linear(u, w, b):
    Bn, S, _ = u.shape
    ub = u.reshape(Bn, S, REC_BLOCKS, REC_BLOCK_W)
    y = jnp.einsum('bsnc,ncd->bsnd', ub, w.astype(jnp.float32))
    return y.reshape(Bn, S, D_REC) + b.astype(jnp.float32)


def rglru_group(xr, yr, conv_w, conv_b, w_a, b_a, w_x, b_x, lam):
    u = causal_depthwise_conv(xr, conv_w, conv_b).astype(jnp.float32)
    r = jax.nn.sigmoid(block_diag_linear(u, w_a, b_a))
    i = jax.nn.sigmoid(block_diag_linear(u, w_x, b_x))
    log_a = -LRU_C * r * jax.nn.softplus(-lam.astype(jnp.float32))
    a = jnp.exp(log_a)
    mult = jnp.sqrt(-jnp.expm1(2.0 * log_a))
    b_in = mult * (i * u)

    def combine(left, right):
        a1, h1 = left
        a2, h2 = right
        return a1 * a2, a2 * h1 + h2

    _, h = lax.associative_scan(combine, (a, b_in), axis=1)
    return (h * jax.nn.gelu(yr.astype(jnp.float32))).astype(xr.dtype)


def forgetting_attention_group(q, k, v, f_logit, b_f, g_q, g_k):
    Bn, S, _ = q.shape

    def heads(t):
        return t.reshape(Bn, S, N_HEADS, HEAD_DIM).transpose(0, 2, 1, 3)

    qh = rms_norm(heads(q), g_q).astype(jnp.float32) * (HEAD_DIM ** -0.5)
    kh = rms_norm(heads(k), g_k).astype(jnp.float32)
    vh = heads(v).astype(jnp.float32)
    log_f = jax.nn.log_sigmoid(f_logit.astype(jnp.float32) + b_f.astype(jnp.float32))
    cum = jnp.cumsum(log_f, axis=1).transpose(0, 2, 1)

    outs = []
    for blk in range(S // Q_BLOCK):
        q0 = blk * Q_BLOCK
        q1 = q0 + Q_BLOCK
        qb = qh[:, :, q0:q1]
        kb = kh[:, :, :q1]
        vb = vh[:, :, :q1]
        s = jnp.einsum('bhqd,bhkd->bhqk', qb, kb)
        s = s + cum[:, :, q0:q1, None] - cum[:, :, None, :q1]
        mask = jnp.arange(q0, q1)[:, None] >= jnp.arange(q1)[None, :]
        s = jnp.where(mask, s, -jnp.inf)
        p = jax.nn.softmax(s, axis=-1)
        outs.append(jnp.einsum('bhqk,bhkd->bhqd', p, vb))
    o = jnp.concatenate(outs, axis=2)
    return o.transpose(0, 2, 1, 3).reshape(Bn, S, D_ATT).astype(q.dtype)


def setup_inputs(seed: int = 0) -> dict:
    key = jax.random.key(seed)
    ks = jax.random.split(key, 24)
    f32 = jnp.float32
    nrm = lambda k, shape, s: jax.random.normal(k, shape, f32) * s
    a0 = jax.random.uniform(ks[10], (DEPTH, D_REC), f32, 0.9, 0.999)
    return {
        'x': jax.random.normal(ks[0], (BATCH, SEQ, D_MODEL), f32),
        'c': jax.random.normal(ks[1], (BATCH, D_MODEL), f32),
        'w_ada': nrm(ks[2], (DEPTH, D_MODEL, 6 * D_MODEL), 0.5 * D_MODEL ** -0.5),
        'b_ada': nrm(ks[3], (DEPTH, 6 * D_MODEL), 0.02),
        'g_mix': 1.0 + nrm(ks[4], (DEPTH, D_MODEL), 0.02),
        'w_in': nrm(ks[5], (DEPTH, D_MODEL, D_IN), D_MODEL ** -0.5),
        'conv_w': nrm(ks[6], (DEPTH, CONV_W, D_REC), CONV_W ** -0.5),
        'conv_b': nrm(ks[7], (DEPTH, D_REC), 0.02),
        'w_gate_a': nrm(ks[8], (DEPTH, REC_BLOCKS, REC_BLOCK_W, REC_BLOCK_W), REC_BLOCK_W ** -0.5),
        'b_gate_a': nrm(ks[9], (DEPTH, D_REC), 0.02),
        'w_gate_x': nrm(ks[11], (DEPTH, REC_BLOCKS, REC_BLOCK_W, REC_BLOCK_W), REC_BLOCK_W ** -0.5),
        'b_gate_x': nrm(ks[12], (DEPTH, D_REC), 0.02),
        'lru_logit': jnp.log(a0) - jnp.log1p(-a0),
        'b_forget': 2.0 + nrm(ks[13], (DEPTH, N_HEADS), 0.5),
        'g_q': 1.0 + nrm(ks[14], (DEPTH, HEAD_DIM), 0.02),
        'g_k': 1.0 + nrm(ks[15], (DEPTH, HEAD_DIM), 0.02),
        'g_out_rec': 1.0 + nrm(ks[16], (DEPTH, D_REC), 0.02),
        'g_out_att': 1.0 + nrm(ks[17], (DEPTH, D_ATT), 0.02),
        'w_out': nrm(ks[18], (DEPTH, D_MIX, D_MODEL), D_MIX ** -0.5),
        'g_ffn': 1.0 + nrm(ks[19], (DEPTH, D_MODEL), 0.02),
        'w_up': nrm(ks[20], (DEPTH, D_MODEL, 2 * D_FF), D_MODEL ** -0.5),
        'w_down': nrm(ks[21], (DEPTH, D_FF, D_MODEL), D_FF ** -0.5),
    }


def reference(x, c, w_ada, b_ada, g_mix, w_in, conv_w, conv_b, w_gate_a, b_gate_a,
              w_gate_x, b_gate_x, lru_logit, b_forget, g_q, g_k, g_out_rec, g_out_att,
              w_out, g_ffn, w_up, w_down):
    c_act = jax.nn.silu(c)
    splits = [D_REC, 2 * D_REC, 2 * D_REC + D_ATT, 2 * D_REC + 2 * D_ATT, 2 * D_REC + 3 * D_ATT]
    for l in range(DEPTH):
        mod = c_act @ w_ada[l] + b_ada[l]
        sh1, sc1, gt1, sh2, sc2, gt2 = jnp.split(mod, 6, axis=-1)

        h = modulate(rms_norm(x, g_mix[l]), sh1, sc1)
        proj = h @ w_in[l]
        xr, yr, q, k, v, f_logit = jnp.split(proj, splits, axis=-1)
        y_rec = rglru_group(xr, yr, conv_w[l], conv_b[l], w_gate_a[l], b_gate_a[l],
                            w_gate_x[l], b_gate_x[l], lru_logit[l])
        y_att = forgetting_attention_group(q, k, v, f_logit, b_forget[l], g_q[l], g_k[l])
        mix = jnp.concatenate([rms_norm(y_rec, g_out_rec[l]), rms_norm(y_att, g_out_att[l])], axis=-1)
        x = x + gt1[:, None, :] * (mix @ w_out[l])

        h2 = modulate(rms_norm(x, g_ffn[l]), sh2, sc2)
        gate, up = jnp.split(h2 @ w_up[l], 2, axis=-1)
        x = x + gt2[:, None, :] * ((jax.nn.silu(gate) * up) @ w_down[l])
    return x
```

```python
import functools
import math

import jax
import jax.numpy as jnp
from jax import lax
from jax.experimental import pallas as pl
from jax.experimental.pallas import tpu as pltpu

F32 = jnp.float32
BF16 = jnp.bfloat16

EPS = 1e-6
LRU_C = 8.0
CONV_W = 4
HEAD_DIM = 128
GATE_BLOCK = 128
LOG2E = math.log2(math.e)

LANES = 128
VMEM_LIMIT_BYTES = 56 * 1024 * 1024

MOD_TN = 1024
INPROJ_TM = 512
LRU_TC = 128
LRU_CB = 256
LRU_TS = 16
ATTN_T = 256
OUT_TM = 512
UP_TM, UP_TN = 1024, 512
DOWN_TM, DOWN_TN = 1024, 512


def _params(n_axes):
    return pltpu.CompilerParams(
        dimension_semantics=("arbitrary",) * n_axes,
        vmem_limit_bytes=VMEM_LIMIT_BYTES,
    )


def _dot(a, b):
    return jnp.dot(a, b, preferred_element_type=F32)


def _rms_scale(x):
    return lax.rsqrt(jnp.mean(x * x, axis=-1, keepdims=True) + EPS)


def _mod_kernel(c_ref, w_ref, b_ref, o_ref):
    c = c_ref[...]
    ca = (c * jax.nn.sigmoid(c)).astype(BF16)
    o_ref[...] = _dot(ca, w_ref[...].astype(BF16)) + b_ref[...]


def _mod(c, w_ada, b_ada):
    bsz, d = c.shape
    n = w_ada.shape[1]
    return pl.pallas_call(
        _mod_kernel,
        grid=(n // MOD_TN,),
        in_specs=[
            pl.BlockSpec((bsz, d), lambda j: (0, 0)),
            pl.BlockSpec((d, MOD_TN), lambda j: (0, j)),
            pl.BlockSpec((1, MOD_TN), lambda j: (0, j)),
        ],
        out_specs=pl.BlockSpec((bsz, MOD_TN), lambda j: (0, j)),
        out_shape=jax.ShapeDtypeStruct((bsz, n), F32),
        compiler_params=_params(1),
        name="mod",
    )(c, w_ada, b_ada.reshape(1, n))


def _inproj_kernel(x_ref, g_ref, sc_ref, sh_ref, w_ref, wf_ref, gq_ref, gk_ref,
                   xr_ref, yr_ref, q_ref, k_ref, v_ref, fl_ref, *, d_rec, d_att, n_heads):
    x = x_ref[...]
    y = x * _rms_scale(x) * g_ref[...]
    h = (y * (1.0 + sc_ref[0]) + sh_ref[0]).astype(BF16)

    xr_ref[...] = _dot(h, w_ref[:, 0:d_rec])
    yr_ref[...] = _dot(h, w_ref[:, d_rec:2 * d_rec])
    fl_ref[...] = _dot(h, wf_ref[...])

    off = 2 * d_rec
    q = _dot(h, w_ref[:, off:off + d_att])
    k = _dot(h, w_ref[:, off + d_att:off + 2 * d_att])
    v = _dot(h, w_ref[:, off + 2 * d_att:off + 3 * d_att])
    q_gain = gq_ref[...] * (HEAD_DIM ** -0.5 * LOG2E)
    k_gain = gk_ref[...]
    for hd in range(n_heads):
        sl = slice(hd * HEAD_DIM, (hd + 1) * HEAD_DIM)
        qh = q[:, sl]
        kh = k[:, sl]
        q_ref[0, hd] = (qh * _rms_scale(qh) * q_gain).astype(BF16)
        k_ref[0, hd] = (kh * _rms_scale(kh) * k_gain).astype(BF16)
        v_ref[0, hd] = v[:, sl].astype(BF16)


def _inproj(x2, g_mix, sc1, sh1, w_main, w_f, g_q, g_k, *, bsz, seq, d_rec, d_att, n_heads):
    t, d = x2.shape
    tm = INPROJ_TM
    spb = seq // tm
    tok = lambda i: (i, 0)
    per_b = lambda i: (i // spb, 0, 0)
    const = lambda i: (0, 0)
    tmaj = lambda i: (i % spb, i // spb)
    hmaj = lambda i: (i // spb, 0, i % spb, 0)
    head_shape = jax.ShapeDtypeStruct((bsz, n_heads, seq, HEAD_DIM), BF16)
    head_spec = pl.BlockSpec((1, n_heads, tm, HEAD_DIM), hmaj)
    return pl.pallas_call(
        functools.partial(_inproj_kernel, d_rec=d_rec, d_att=d_att, n_heads=n_heads),
        grid=(t // tm,),
        in_specs=[
            pl.BlockSpec((tm, d), tok),
            pl.BlockSpec((1, d), const),
            pl.BlockSpec((1, 1, d), per_b),
            pl.BlockSpec((1, 1, d), per_b),
            pl.BlockSpec(w_main.shape, const, pipeline_mode=pl.Buffered(1)),
            pl.BlockSpec(w_f.shape, const, pipeline_mode=pl.Buffered(1)),
            pl.BlockSpec((1, HEAD_DIM), const),
            pl.BlockSpec((1, HEAD_DIM), const),
        ],
        out_specs=[
            pl.BlockSpec((tm, d_rec), tmaj),
            pl.BlockSpec((tm, d_rec), tmaj),
            head_spec, head_spec, head_spec,
            pl.BlockSpec((tm, LANES), tok),
        ],
        out_shape=[
            jax.ShapeDtypeStruct((seq, bsz * d_rec), F32),
            jax.ShapeDtypeStruct((seq, bsz * d_rec), F32),
            head_shape, head_shape, head_shape,
            jax.ShapeDtypeStruct((t, LANES), F32),
        ],
        compiler_params=_params(1),
        name="inproj",
    )(x2, g_mix, sc1, sh1, w_main, w_f, g_q, g_k)


def _cum_kernel(f_ref, b_ref, o_ref):
    z = f_ref[...] + b_ref[...]
    x = -(jnp.maximum(-z, 0.0) + jnp.log1p(jnp.exp(-jnp.abs(z))))
    n = x.shape[-1]
    lane = lax.broadcasted_iota(jnp.int32, x.shape, 1)
    d = 1
    while d < n:
        x = x + jnp.where(lane >= d, pltpu.roll(x, d, axis=1), 0.0)
        d *= 2
    o_ref[...] = x


def _cum(f_rows, b_rows):
    r, n = f_rows.shape
    return pl.pallas_call(
        _cum_kernel,
        grid=(1,),
        in_specs=[pl.BlockSpec((r, n), lambda i: (0, 0)),
                  pl.BlockSpec((r, 1), lambda i: (0, 0))],
        out_specs=pl.BlockSpec((r, n), lambda i: (0, 0)),
        out_shape=jax.ShapeDtypeStruct((r, n), F32),
        compiler_params=_params(1),
        name="cum",
    )(f_rows, b_rows)


def _neg_expm1_2x(x, exp_x):
    return jnp.tanh(-x) * (1.0 + exp_x * exp_x)


def _lru_kernel(xr_ref, yr_ref, cw_ref, cb_ref, wax_ref, bax_ref, lam_ref, o_ref,
                xp_ref, h_ref):
    tc = xr_ref.shape[0]
    pad = 8
    hist = CONV_W - 1
    t_idx = pl.program_id(1)

    @pl.when(t_idx == 0)
    def _():
        xp_ref[0:pad] = jnp.zeros((pad,) + xp_ref.shape[1:], F32)
        h_ref[...] = jnp.zeros(h_ref.shape, F32)

    @pl.when(t_idx > 0)
    def _():
        xp_ref[pad - hist:pad] = xp_ref[pad + tc - hist:pad + tc]

    xp_ref[pad:pad + tc] = xr_ref[...]

    lam = lam_ref[...]
    softplus_neg = jnp.maximum(-lam, 0.0) + jnp.log1p(jnp.exp(-jnp.abs(lam)))
    log_a_gain = -LRU_C * softplus_neg
    n_sub = tc // LRU_TS
    n_lane_blocks = xr_ref.shape[2] // GATE_BLOCK
    bsz = xr_ref.shape[1]

    def sub_chunk(j, carry):
        t0 = j * LRU_TS
        for nb in range(n_lane_blocks):
            ln = slice(nb * GATE_BLOCK, (nb + 1) * GATE_BLOCK)
            base = pad + t0 - hist
            u = xp_ref[pl.ds(base, LRU_TS), :, ln] * cw_ref[0:1, ln]
            for kk in range(1, CONV_W):
                u = u + xp_ref[pl.ds(base + kk, LRU_TS), :, ln] * cw_ref[kk:kk + 1, ln]
            u = u + cb_ref[:, ln]
            u2 = u.reshape(LRU_TS * bsz, GATE_BLOCK).astype(BF16)
            g = _dot(u2, wax_ref[nb]) + bax_ref[nb]
            r = jax.nn.sigmoid(g[:, :GATE_BLOCK]).reshape(LRU_TS, bsz, GATE_BLOCK)
            i = jax.nn.sigmoid(g[:, GATE_BLOCK:]).reshape(LRU_TS, bsz, GATE_BLOCK)
            log_a = r * log_a_gain[:, ln]
            a = jnp.exp(log_a)
            b_in = jnp.sqrt(_neg_expm1_2x(log_a, a)) * (i * u)
            h = h_ref[:, ln]
            hs = []
            for tt in range(LRU_TS):
                h = a[tt] * h + b_in[tt]
                hs.append(h)
            h_ref[:, ln] = h
            gate = jax.nn.gelu(yr_ref[pl.ds(t0, LRU_TS), :, ln])
            o_ref[pl.ds(t0, LRU_TS), :, ln] = (jnp.stack(hs, axis=0) * gate).astype(BF16)
        return carry

    lax.fori_loop(0, n_sub, sub_chunk, 0)


def _lru(xr3, yr3, conv_w, conv_b, wax, bax, lam):
    seq, bsz, c = xr3.shape
    tc, cb = LRU_TC, LRU_CB
    nblk = cb // GATE_BLOCK
    blk = lambda ci, ti: (ti, 0, ci)
    chan = lambda ci, ti: (0, ci)
    return pl.pallas_call(
        _lru_kernel,
        grid=(c // cb, seq // tc),
        in_specs=[
            pl.BlockSpec((tc, bsz, cb), blk),
            pl.BlockSpec((tc, bsz, cb), blk),
            pl.BlockSpec((CONV_W, cb), chan),
            pl.BlockSpec((1, cb), chan),
            pl.BlockSpec((nblk, GATE_BLOCK, 2 * GATE_BLOCK), lambda ci, ti: (ci, 0, 0)),
            pl.BlockSpec((nblk, 1, 2 * GATE_BLOCK), lambda ci, ti: (ci, 0, 0)),
            pl.BlockSpec((1, cb), chan),
        ],
        out_specs=pl.BlockSpec((tc, bsz, cb), blk),
        out_shape=jax.ShapeDtypeStruct((seq, bsz, c), BF16),
        scratch_shapes=[
            pltpu.VMEM((tc + 8, bsz, cb), F32),
            pltpu.VMEM((bsz, cb), F32),
        ],
        compiler_params=_params(2),
        name="lru",
    )(xr3, yr3, conv_w, conv_b, wax, bax, lam)


def _attn_kernel(q_ref, k_ref, v_ref, cum_ref, o_ref):
    seq = q_ref.shape[2]
    t = ATTN_T
    nq = seq // t
    row = lax.broadcasted_iota(jnp.int32, (t, t), 0)
    col = lax.broadcasted_iota(jnp.int32, (t, t), 1)
    causal = col <= row

    for qi in range(nq):
        q = q_ref[0, 0, qi * t:(qi + 1) * t, :]
        c0 = cum_ref[0, 0, qi][:, 0:1]

        def block(kj, carry, masked):
            m, l, acc = carry
            ks = pl.multiple_of(kj * t, t)
            kb = k_ref[0, 0, pl.ds(ks, t), :]
            vb = v_ref[0, 0, pl.ds(ks, t), :]
            bias = (c0 - cum_ref[0, 0, kj]) * LOG2E
            s = lax.dot_general(q, kb, (((1,), (1,)), ((), ())),
                                preferred_element_type=F32) + bias
            if masked:
                s = jnp.where(causal, s, -jnp.inf)
            m_new = jnp.maximum(m, jnp.max(s, axis=1, keepdims=True))
            alpha = jnp.exp2(m - m_new)
            p = jnp.exp2(s - m_new)
            l_new = alpha * l + jnp.sum(p, axis=1, keepdims=True)
            acc_new = alpha * acc + _dot(p.astype(BF16), vb)
            return m_new, l_new, acc_new

        carry = (jnp.full((t, 1), -jnp.inf, F32), jnp.zeros((t, 1), F32),
                 jnp.zeros((t, HEAD_DIM), F32))
        if qi > 0:
            carry = lax.fori_loop(0, qi, functools.partial(block, masked=False), carry)
        _, l, acc = block(qi, carry, True)
        o_ref[0, 0, qi * t:(qi + 1) * t, :] = (acc / l).astype(BF16)


def _attn(q, k, v, cum5):
    bsz, n_heads, seq, dh = q.shape
    spec = pl.BlockSpec((1, 1, seq, dh), lambda b, h: (b, h, 0, 0))
    return pl.pallas_call(
        _attn_kernel,
        grid=(bsz, n_heads),
        in_specs=[spec, spec, spec,
                  pl.BlockSpec((1, 1) + cum5.shape[2:], lambda b, h: (b, h, 0, 0, 0))],
        out_specs=spec,
        out_shape=jax.ShapeDtypeStruct(q.shape, BF16),
        compiler_params=_params(2),
        name="attn",
    )(q, k, v, cum5)


def _outproj_kernel(yrec_ref, yatt_ref, x_ref, w_ref, gor_ref, goa_ref, gt_ref,
                    gf_ref, sc_ref, sh_ref, x1_ref, h2_ref, *, n_heads):
    yr = yrec_ref[...].astype(F32)
    parts = [(yr * _rms_scale(yr) * gor_ref[...]).astype(BF16)]
    heads = [yatt_ref[0, hd].astype(F32) for hd in range(n_heads)]
    ssq = heads[0] * heads[0]
    for hh in heads[1:]:
        ssq = ssq + hh * hh
    inv = lax.rsqrt(jnp.sum(ssq, axis=-1, keepdims=True) / (n_heads * HEAD_DIM) + EPS)
    for hd in range(n_heads):
        gain = goa_ref[:, hd * HEAD_DIM:(hd + 1) * HEAD_DIM]
        parts.append((heads[hd] * inv * gain).astype(BF16))
    mix = jnp.concatenate(parts, axis=1)
    x1 = x_ref[...] + gt_ref[0] * _dot(mix, w_ref[...])
    x1_ref[...] = x1
    y = x1 * _rms_scale(x1) * gf_ref[...]
    h2_ref[...] = (y * (1.0 + sc_ref[0]) + sh_ref[0]).astype(BF16)


def _outproj(y_rec2, y_att, x2, w_out, g_or, g_oa, gt1, g_ffn, sc2, sh2, *, seq):
    t, d = x2.shape
    bsz, n_heads, _, dh = y_att.shape
    d_rec = y_rec2.shape[1] // bsz
    tm = OUT_TM
    spb = seq // tm
    tok = lambda i: (i, 0)
    per_b = lambda i: (i // spb, 0, 0)
    const = lambda i: (0, 0)
    return pl.pallas_call(
        functools.partial(_outproj_kernel, n_heads=n_heads),
        grid=(t // tm,),
        in_specs=[
            pl.BlockSpec((tm, d_rec), lambda i: (i % spb, i // spb)),
            pl.BlockSpec((1, n_heads, tm, dh), lambda i: (i // spb, 0, i % spb, 0)),
            pl.BlockSpec((tm, d), tok),
            pl.BlockSpec(w_out.shape, const, pipeline_mode=pl.Buffered(1)),
            pl.BlockSpec((1, d_rec), const),
            pl.BlockSpec((1, n_heads * dh), const),
            pl.BlockSpec((1, 1, d), per_b),
            pl.BlockSpec((1, d), const),
            pl.BlockSpec((1, 1, d), per_b),
            pl.BlockSpec((1, 1, d), per_b),
        ],
        out_specs=[pl.BlockSpec((tm, d), tok), pl.BlockSpec((tm, d), tok)],
        out_shape=[jax.ShapeDtypeStruct((t, d), F32), jax.ShapeDtypeStruct((t, d), BF16)],
        compiler_params=_params(1),
        name="outproj",
    )(y_rec2, y_att, x2, w_out, g_or, g_oa, gt1, g_ffn, sc2, sh2)


def _up_kernel(h_ref, wg_ref, wu_ref, a_ref):
    h = h_ref[...]
    g = _dot(h, wg_ref[...])
    u = _dot(h, wu_ref[...])
    a_ref[...] = (g * jax.nn.sigmoid(g) * u).astype(BF16)


def _up(h2, w_up, d_ff):
    t, d = h2.shape
    tm, tn = UP_TM, UP_TN
    nj = d_ff // tn
    return pl.pallas_call(
        _up_kernel,
        grid=(t // tm, nj),
        in_specs=[
            pl.BlockSpec((tm, d), lambda i, j: (i, 0)),
            pl.BlockSpec((d, tn), lambda i, j: (0, j)),
            pl.BlockSpec((d, tn), lambda i, j: (0, j + nj)),
        ],
        out_specs=pl.BlockSpec((tm, tn), lambda i, j: (i, j)),
        out_shape=jax.ShapeDtypeStruct((t, d_ff), BF16),
        compiler_params=_params(2),
        name="up",
    )(h2, w_up, w_up)


def _down_kernel(a_ref, w_ref, x1_ref, gt_ref, o_ref):
    o_ref[...] = x1_ref[...] + gt_ref[0] * _dot(a_ref[...], w_ref[...])


def _down(a, w_down, x1, gt2, *, seq):
    t, d_ff = a.shape
    d = w_down.shape[1]
    tm, tn = DOWN_TM, DOWN_TN
    spb = seq // tm
    return pl.pallas_call(
        _down_kernel,
        grid=(t // tm, d // tn),
        in_specs=[
            pl.BlockSpec((tm, d_ff), lambda i, j: (i, 0)),
            pl.BlockSpec((d_ff, tn), lambda i, j: (0, j)),
            pl.BlockSpec((tm, tn), lambda i, j: (i, j)),
            pl.BlockSpec((1, 1, tn), lambda i, j: (i // spb, 0, j)),
        ],
        out_specs=pl.BlockSpec((tm, tn), lambda i, j: (i, j)),
        out_shape=jax.ShapeDtypeStruct((t, d), F32),
        compiler_params=_params(2),
        name="down",
    )(a, w_down, x1, gt2)


def kernel(x, c, w_ada, b_ada, g_mix, w_in, conv_w, conv_b, w_gate_a, b_gate_a, w_gate_x, b_gate_x,
           lru_logit, b_forget, g_q, g_k, g_out_rec, g_out_att, w_out, g_ffn, w_up, w_down):
    bsz, seq, d = x.shape
    depth = w_ada.shape[0]
    d_rec = conv_w.shape[-1]
    n_heads = b_forget.shape[-1]
    d_att = n_heads * HEAD_DIM
    n_gate_blocks = w_gate_a.shape[1]
    d_ff = w_down.shape[1]
    t = bsz * seq
    row = lambda v: v.reshape(1, -1)

    x2 = x.reshape(t, d)
    for l in range(depth):
        mod = _mod(c, w_ada[l], b_ada[l])
        sh1, sc1, gt1, sh2, sc2, gt2 = [m.reshape(bsz, 1, d) for m in jnp.split(mod, 6, axis=-1)]

        w_main = w_in[l][:, :2 * d_rec + 3 * d_att].astype(BF16)
        w_f = jnp.pad(w_in[l][:, 2 * d_rec + 3 * d_att:], ((0, 0), (0, LANES - n_heads))).astype(BF16)
        xr, yr, q, k, v, fl = _inproj(x2, row(g_mix[l]), sc1, sh1, w_main, w_f, row(g_q[l]), row(g_k[l]),
                                      bsz=bsz, seq=seq, d_rec=d_rec, d_att=d_att, n_heads=n_heads)

        f_rows = fl[:, :n_heads].reshape(bsz, seq, n_heads).transpose(0, 2, 1).reshape(bsz * n_heads, seq)
        b_rows = jnp.tile(b_forget[l], bsz).reshape(bsz * n_heads, 1)
        cum = _cum(f_rows, b_rows).reshape(bsz, n_heads, seq // ATTN_T, 1, ATTN_T)

        wax = jnp.concatenate([w_gate_a[l], w_gate_x[l]], axis=-1).astype(BF16)
        bax = jnp.concatenate([b_gate_a[l].reshape(n_gate_blocks, 1, GATE_BLOCK),
                               b_gate_x[l].reshape(n_gate_blocks, 1, GATE_BLOCK)], axis=-1)
        y_rec = _lru(xr.reshape(seq, bsz, d_rec), yr.reshape(seq, bsz, d_rec), conv_w[l], row(conv_b[l]),
                     wax, bax, row(lru_logit[l]))
        y_att = _attn(q, k, v, cum)

        x1, h2 = _outproj(y_rec.reshape(seq, bsz * d_rec), y_att, x2, w_out[l].astype(BF16),
                          row(g_out_rec[l]), row(g_out_att[l]), gt1, row(g_ffn[l]), sc2, sh2, seq=seq)
        a = _up(h2, w_up[l].astype(BF16), d_ff)
        x2 = _down(a, w_down[l].astype(BF16), x1, gt2, seq=seq)
    return x2.reshape(bsz, seq, d)
```

```python
import functools
import math

import jax
import jax.numpy as jnp
from jax import lax
from jax.experimental import pallas as pl
from jax.experimental.pallas import tpu as pltpu

F32 = jnp.float32
BF16 = jnp.bfloat16

EPS = 1e-6
LRU_C = 8.0
CONV_W = 4
HEAD_DIM = 128
GATE_BLOCK = 128
LOG2E = math.log2(math.e)

LANES = 128
VMEM_LIMIT_BYTES = 56 * 1024 * 1024

MOD_TN = 1024
INPROJ_TM = 512
LRU_TC = 128
LRU_CB = 256
LRU_TS = 16
ATTN_T = 256
OUT_TM = 512
UP_TM, UP_TN = 1024, 512
DOWN_TM, DOWN_TN = 1024, 512


def _params(n_axes):
    return pltpu.CompilerParams(
        dimension_semantics=("arbitrary",) * n_axes,
        vmem_limit_bytes=VMEM_LIMIT_BYTES,
    )


def _dot(a, b):
    return jnp.dot(a, b, preferred_element_type=F32)


def _rms_scale(x):
    return lax.rsqrt(jnp.mean(x * x, axis=-1, keepdims=True) + EPS)


def _mod_kernel(c_ref, w_ref, b_ref, o_ref):
    c = c_ref[...]
    ca = (c * jax.nn.sigmoid(c)).astype(BF16)
    o_ref[...] = _dot(ca, w_ref[...].astype(BF16)) + b_ref[...]


def _mod(c, w_ada, b_ada):
    bsz, d = c.shape
    n = w_ada.shape[1]
    return pl.pallas_call(
        _mod_kernel,
        grid=(n // MOD_TN,),
        in_specs=[
            pl.BlockSpec((bsz, d), lambda j: (0, 0)),
            pl.BlockSpec((d, MOD_TN), lambda j: (0, j)),
            pl.BlockSpec((1, MOD_TN), lambda j: (0, j)),
        ],
        out_specs=pl.BlockSpec((bsz, MOD_TN), lambda j: (0, j)),
        out_shape=jax.ShapeDtypeStruct((bsz, n), F32),
        compiler_params=_params(1),
        name="mod",
    )(c, w_ada, b_ada.reshape(1, n))


def _inproj_kernel(x_ref, g_ref, sc_ref, sh_ref, w_ref, wf_ref, gq_ref, gk_ref,
                   xr_ref, yr_ref, q_ref, k_ref, v_ref, fl_ref, *, d_rec, d_att, n_heads):
    x = x_ref[...]
    y = x * _rms_scale(x) * g_ref[...]
    h = (y * (1.0 + sc_ref[0]) + sh_ref[0]).astype(BF16)

    xr_ref[...] = _dot(h, w_ref[:, 0:d_rec])
    yr_ref[...] = _dot(h, w_ref[:, d_rec:2 * d_rec])
    fl_ref[...] = _dot(h, wf_ref[...])

    off = 2 * d_rec
    q = _dot(h, w_ref[:, off:off + d_att])
    k = _dot(h, w_ref[:, off + d_att:off + 2 * d_att])
    v = _dot(h, w_ref[:, off + 2 * d_att:off + 3 * d_att])
    q_gain = gq_ref[...] * (HEAD_DIM ** -0.5 * LOG2E)
    k_gain = gk_ref[...]
    for hd in range(n_heads):
        sl = slice(hd * HEAD_DIM, (hd + 1) * HEAD_DIM)
        qh = q[:, sl]
        kh = k[:, sl]
        q_ref[0, hd] = (qh * _rms_scale(qh) * q_gain).astype(BF16)
        k_ref[0, hd] = (kh * _rms_scale(kh) * k_gain).astype(BF16)
        v_ref[0, hd] = v[:, sl].astype(BF16)


def _inproj(x2, g_mix, sc1, sh1, w_main, w_f, g_q, g_k, *, bsz, seq, d_rec, d_att, n_heads):
    t, d = x2.shape
    tm = INPROJ_TM
    spb = seq // tm
    tok = lambda i: (i, 0)
    per_b = lambda i: (i // spb, 0, 0)
    const = lambda i: (0, 0)
    tmaj = lambda i: (i % spb, i // spb)
    hmaj = lambda i: (i // spb, 0, i % spb, 0)
    head_shape = jax.ShapeDtypeStruct((bsz, n_heads, seq, HEAD_DIM), BF16)
    head_spec = pl.BlockSpec((1, n_heads, tm, HEAD_DIM), hmaj)
    return pl.pallas_call(
        functools.partial(_inproj_kernel, d_rec=d_rec, d_att=d_att, n_heads=n_heads),
        grid=(t // tm,),
        in_specs=[
            pl.BlockSpec((tm, d), tok),
            pl.BlockSpec((1, d), const),
            pl.BlockSpec((1, 1, d), per_b),
            pl.BlockSpec((1, 1, d), per_b),
            pl.BlockSpec(w_main.shape, const, pipeline_mode=pl.Buffered(1)),
            pl.BlockSpec(w_f.shape, const, pipeline_mode=pl.Buffered(1)),
            pl.BlockSpec((1, HEAD_DIM), const),
            pl.BlockSpec((1, HEAD_DIM), const),
        ],
        out_specs=[
            pl.BlockSpec((tm, d_rec), tmaj),
            pl.BlockSpec((tm, d_rec), tmaj),
            head_spec, head_spec, head_spec,
            pl.BlockSpec((tm, LANES), tok),
        ],
        out_shape=[
            jax.ShapeDtypeStruct((seq, bsz * d_rec), F32),
            jax.ShapeDtypeStruct((seq, bsz * d_rec), F32),
            head_shape, head_shape, head_shape,
            jax.ShapeDtypeStruct((t, LANES), F32),
        ],
        compiler_params=_params(1),
        name="inproj",
    )(x2, g_mix, sc1, sh1, w_main, w_f, g_q, g_k)


def _cum_kernel(f_ref, b_ref, o_ref):
    z = f_ref[...] + b_ref[...]
    x = -(jnp.maximum(-z, 0.0) + jnp.log1p(jnp.exp(-jnp.abs(z))))
    n = x.shape[-1]
    lane = lax.broadcasted_iota(jnp.int32, x.shape, 1)
    d = 1
    while d < n:
        x = x + jnp.where(lane >= d, pltpu.roll(x, d, axis=1), 0.0)
        d *= 2
    o_ref[...] = x


def _cum(f_rows, b_rows):
    r, n = f_rows.shape
    return pl.pallas_call(
        _cum_kernel,
        grid=(1,),
        in_specs=[pl.BlockSpec((r, n), lambda i: (0, 0)),
                  pl.BlockSpec((r, 1), lambda i: (0, 0))],
        out_specs=pl.BlockSpec((r, n), lambda i: (0, 0)),
        out_shape=jax.ShapeDtypeStruct((r, n), F32),
        compiler_params=_params(1),
        name="cum",
    )(f_rows, b_rows)


def _neg_expm1_2x(x, exp_x):
    return jnp.tanh(-x) * (1.0 + exp_x * exp_x)


def _lru_kernel(xr_ref, yr_ref, cw_ref, cb_ref, wax_ref, bax_ref, lam_ref, o_ref,
                xp_ref, h_ref):
    tc = xr_ref.shape[0]
    pad = 8
    hist = CONV_W - 1
    t_idx = pl.program_id(1)

    @pl.when(t_idx == 0)
    def _():
        xp_ref[0:pad] = jnp.zeros((pad,) + xp_ref.shape[1:], F32)
        h_ref[...] = jnp.zeros(h_ref.shape, F32)

    @pl.when(t_idx > 0)
    def _():
        xp_ref[pad - hist:pad] = xp_ref[pad + tc - hist:pad + tc]

    xp_ref[pad:pad + tc] = xr_ref[...]

    lam = lam_ref[...]
    softplus_neg = jnp.maximum(-lam, 0.0) + jnp.log1p(jnp.exp(-jnp.abs(lam)))
    log_a_gain = -LRU_C * softplus_neg
    n_sub = tc // LRU_TS
    n_lane_blocks = xr_ref.shape[2] // GATE_BLOCK
    bsz = xr_ref.shape[1]

    def sub_chunk(j, carry):
        t0 = j * LRU_TS
        for nb in range(n_lane_blocks):
            ln = slice(nb * GATE_BLOCK, (nb + 1) * GATE_BLOCK)
            base = pad + t0 - hist
            u = xp_ref[pl.ds(base, LRU_TS), :, ln] * cw_ref[0:1, ln]
            for kk in range(1, CONV_W):
                u = u + xp_ref[pl.ds(base + kk, LRU_TS), :, ln] * cw_ref[kk:kk + 1, ln]
            u = u + cb_ref[:, ln]
            u2 = u.reshape(LRU_TS * bsz, GATE_BLOCK).astype(BF16)
            g = _dot(u2, wax_ref[nb]) + bax_ref[nb]
            r = jax.nn.sigmoid(g[:, :GATE_BLOCK]).reshape(LRU_TS, bsz, GATE_BLOCK)
            i = jax.nn.sigmoid(g[:, GATE_BLOCK:]).reshape(LRU_TS, bsz, GATE_BLOCK)
            log_a = r * log_a_gain[:, ln]
            a = jnp.exp(log_a)
            b_in = jnp.sqrt(_neg_expm1_2x(log_a, a)) * (i * u)
            h = h_ref[:, ln]
            hs = []
            for tt in range(LRU_TS):
                h = a[tt] * h + b_in[tt]
                hs.append(h)
            h_ref[:, ln] = h
            gate = jax.nn.gelu(yr_ref[pl.ds(t0, LRU_TS), :, ln])
            o_ref[pl.ds(t0, LRU_TS), :, ln] = (jnp.stack(hs, axis=0) * gate).astype(BF16)
        return carry

    lax.fori_loop(0, n_sub, sub_chunk, 0)


def _lru(xr3, yr3, conv_w, conv_b, wax, bax, lam):
    seq, bsz, c = xr3.shape
    tc, cb = LRU_TC, LRU_CB
    nblk = cb // GATE_BLOCK
    blk = lambda ci, ti: (ti, 0, ci)
    chan = lambda ci, ti: (0, ci)
    return pl.pallas_call(
        _lru_kernel,
        grid=(c // cb, seq // tc),
        in_specs=[
            pl.BlockSpec((tc, bsz, cb), blk),
            pl.BlockSpec((tc, bsz, cb), blk),
            pl.BlockSpec((CONV_W, cb), chan),
            pl.BlockSpec((1, cb), chan),
            pl.BlockSpec((nblk, GATE_BLOCK, 2 * GATE_BLOCK), lambda ci, ti: (ci, 0, 0)),
            pl.BlockSpec((nblk, 1, 2 * GATE_BLOCK), lambda ci, ti: (ci, 0, 0)),
            pl.BlockSpec((1, cb), chan),
        ],
        out_specs=pl.BlockSpec((tc, bsz, cb), blk),
        out_shape=jax.ShapeDtypeStruct((seq, bsz, c), BF16),
        scratch_shapes=[
            pltpu.VMEM((tc + 8, bsz, cb), F32),
            pltpu.VMEM((bsz, cb), F32),
        ],
        compiler_params=_params(2),
        name="lru",
    )(xr3, yr3, conv_w, conv_b, wax, bax, lam)


def _attn_kernel(q_ref, k_ref, v_ref, cum_ref, o_ref):
    seq = q_ref.shape[2]
    t = ATTN_T
    nq = seq // t
    row = lax.broadcasted_iota(jnp.int32, (t, t), 0)
    col = lax.broadcasted_iota(jnp.int32, (t, t), 1)
    causal = col <= row

    state = [None] * nq
    for kj in range(nq):
        kb = k_ref[0, 0, kj * t:(kj + 1) * t, :]
        vb = v_ref[0, 0, kj * t:(kj + 1) * t, :]
        cum_k = cum_ref[0, 0, kj]
        for qi in range(kj, nq):
            q = q_ref[0, 0, qi * t:(qi + 1) * t, :]
            c0 = cum_ref[0, 0, qi][:, 0:1]
            s = lax.dot_general(q, kb, (((1,), (1,)), ((), ())),
                                preferred_element_type=F32) + (c0 - cum_k) * LOG2E
            if qi == kj:
                s = jnp.where(causal, s, -jnp.inf)
            m_blk = jnp.max(s, axis=1, keepdims=True)
            if state[qi] is None:
                m_new = m_blk
                p = jnp.exp2(s - m_new)
                l_new = jnp.sum(p, axis=1, keepdims=True)
                acc_new = _dot(p.astype(BF16), vb)
            else:
                m, l, acc = state[qi]
                m_new = jnp.maximum(m, m_blk)
                alpha = jnp.exp2(m - m_new)
                p = jnp.exp2(s - m_new)
                l_new = alpha * l + jnp.sum(p, axis=1, keepdims=True)
                acc_new = alpha * acc + _dot(p.astype(BF16), vb)
            state[qi] = (m_new, l_new, acc_new)
            if qi == kj:
                o_ref[0, 0, qi * t:(qi + 1) * t, :] = (acc_new / l_new).astype(BF16)


def _attn(q, k, v, cum5):
    bsz, n_heads, seq, dh = q.shape
    spec = pl.BlockSpec((1, 1, seq, dh), lambda b, h: (b, h, 0, 0))
    return pl.pallas_call(
        _attn_kernel,
        grid=(bsz, n_heads),
        in_specs=[spec, spec, spec,
                  pl.BlockSpec((1, 1) + cum5.shape[2:], lambda b, h: (b, h, 0, 0, 0))],
        out_specs=spec,
        out_shape=jax.ShapeDtypeStruct(q.shape, BF16),
        compiler_params=_params(2),
        name="attn",
    )(q, k, v, cum5)


def _outproj_kernel(yrec_ref, yatt_ref, x_ref, w_ref, gor_ref, goa_ref, gt_ref,
                    gf_ref, sc_ref, sh_ref, x1_ref, h2_ref, *, n_heads):
    yr = yrec_ref[...].astype(F32)
    parts = [(yr * _rms_scale(yr) * gor_ref[...]).astype(BF16)]
    heads = [yatt_ref[0, hd].astype(F32) for hd in range(n_heads)]
    ssq = heads[0] * heads[0]
    for hh in heads[1:]:
        ssq = ssq + hh * hh
    inv = lax.rsqrt(jnp.sum(ssq, axis=-1, keepdims=True) / (n_heads * HEAD_DIM) + EPS)
    for hd in range(n_heads):
        gain = goa_ref[:, hd * HEAD_DIM:(hd + 1) * HEAD_DIM]
        parts.append((heads[hd] * inv * gain).astype(BF16))
    mix = jnp.concatenate(parts, axis=1)
    x1 = x_ref[...] + gt_ref[0] * _dot(mix, w_ref[...])
    x1_ref[...] = x1
    y = x1 * _rms_scale(x1) * gf_ref[...]
    h2_ref[...] = (y * (1.0 + sc_ref[0]) + sh_ref[0]).astype(BF16)


def _outproj(y_rec2, y_att, x2, w_out, g_or, g_oa, gt1, g_ffn, sc2, sh2, *, seq):
    t, d = x2.shape
    bsz, n_heads, _, dh = y_att.shape
    d_rec = y_rec2.shape[1] // bsz
    tm = OUT_TM
    spb = seq // tm
    tok = lambda i: (i, 0)
    per_b = lambda i: (i // spb, 0, 0)
    const = lambda i: (0, 0)
    return pl.pallas_call(
        functools.partial(_outproj_kernel, n_heads=n_heads),
        grid=(t // tm,),
        in_specs=[
            pl.BlockSpec((tm, d_rec), lambda i: (i % spb, i // spb)),
            pl.BlockSpec((1, n_heads, tm, dh), lambda i: (i // spb, 0, i % spb, 0)),
            pl.BlockSpec((tm, d), tok),
            pl.BlockSpec(w_out.shape, const, pipeline_mode=pl.Buffered(1)),
            pl.BlockSpec((1, d_rec), const),
            pl.BlockSpec((1, n_heads * dh), const),
            pl.BlockSpec((1, 1, d), per_b),
            pl.BlockSpec((1, d), const),
            pl.BlockSpec((1, 1, d), per_b),
            pl.BlockSpec((1, 1, d), per_b),
        ],
        out_specs=[pl.BlockSpec((tm, d), tok), pl.BlockSpec((tm, d), tok)],
        out_shape=[jax.ShapeDtypeStruct((t, d), F32), jax.ShapeDtypeStruct((t, d), BF16)],
        compiler_params=_params(1),
        name="outproj",
    )(y_rec2, y_att, x2, w_out, g_or, g_oa, gt1, g_ffn, sc2, sh2)


def _up_kernel(h_ref, wg_ref, wu_ref, a_ref):
    h = h_ref[...]
    g = _dot(h, wg_ref[...])
    u = _dot(h, wu_ref[...])
    a_ref[...] = (g * jax.nn.sigmoid(g) * u).astype(BF16)


def _up(h2, w_up, d_ff):
    t, d = h2.shape
    tm, tn = UP_TM, UP_TN
    nj = d_ff // tn
    return pl.pallas_call(
        _up_kernel,
        grid=(t // tm, nj),
        in_specs=[
            pl.BlockSpec((tm, d), lambda i, j: (i, 0)),
            pl.BlockSpec((d, tn), lambda i, j: (0, j)),
            pl.BlockSpec((d, tn), lambda i, j: (0, j + nj)),
        ],
        out_specs=pl.BlockSpec((tm, tn), lambda i, j: (i, j)),
        out_shape=jax.ShapeDtypeStruct((t, d_ff), BF16),
        compiler_params=_params(2),
        name="up",
    )(h2, w_up, w_up)


def _down_kernel(a_ref, w_ref, x1_ref, gt_ref, o_ref):
    o_ref[...] = x1_ref[...] + gt_ref[0] * _dot(a_ref[...], w_ref[...])


def _down(a, w_down, x1, gt2, *, seq):
    t, d_ff = a.shape
    d = w_down.shape[1]
    tm, tn = DOWN_TM, DOWN_TN
    spb = seq // tm
    return pl.pallas_call(
        _down_kernel,
        grid=(t // tm, d // tn),
        in_specs=[
            pl.BlockSpec((tm, d_ff), lambda i, j: (i, 0)),
            pl.BlockSpec((d_ff, tn), lambda i, j: (0, j)),
            pl.BlockSpec((tm, tn), lambda i, j: (i, j)),
            pl.BlockSpec((1, 1, tn), lambda i, j: (i // spb, 0, j)),
        ],
        out_specs=pl.BlockSpec((tm, tn), lambda i, j: (i, j)),
        out_shape=jax.ShapeDtypeStruct((t, d), F32),
        compiler_params=_params(2),
        name="down",
    )(a, w_down, x1, gt2)


def kernel(x, c, w_ada, b_ada, g_mix, w_in, conv_w, conv_b, w_gate_a, b_gate_a, w_gate_x, b_gate_x,
           lru_logit, b_forget, g_q, g_k, g_out_rec, g_out_att, w_out, g_ffn, w_up, w_down):
    bsz, seq, d = x.shape
    depth = w_ada.shape[0]
    d_rec = conv_w.shape[-1]
    n_heads = b_forget.shape[-1]
    d_att = n_heads * HEAD_DIM
    n_gate_blocks = w_gate_a.shape[1]
    d_ff = w_down.shape[1]
    t = bsz * seq
    row = lambda v: v.reshape(1, -1)

    x2 = x.reshape(t, d)
    for l in range(depth):
        mod = _mod(c, w_ada[l], b_ada[l])
        sh1, sc1, gt1, sh2, sc2, gt2 = [m.reshape(bsz, 1, d) for m in jnp.split(mod, 6, axis=-1)]

        w_main = w_in[l][:, :2 * d_rec + 3 * d_att].astype(BF16)
        w_f = jnp.pad(w_in[l][:, 2 * d_rec + 3 * d_att:], ((0, 0), (0, LANES - n_heads))).astype(BF16)
        xr, yr, q, k, v, fl = _inproj(x2, row(g_mix[l]), sc1, sh1, w_main, w_f, row(g_q[l]), row(g_k[l]),
                                      bsz=bsz, seq=seq, d_rec=d_rec, d_att=d_att, n_heads=n_heads)

        f_rows = fl[:, :n_heads].reshape(bsz, seq, n_heads).transpose(0, 2, 1).reshape(bsz * n_heads, seq)
        b_rows = jnp.tile(b_forget[l], bsz).reshape(bsz * n_heads, 1)
        cum = _cum(f_rows, b_rows).reshape(bsz, n_heads, seq // ATTN_T, 1, ATTN_T)

        wax = jnp.concatenate([w_gate_a[l], w_gate_x[l]], axis=-1).astype(BF16)
        bax = jnp.concatenate([b_gate_a[l].reshape(n_gate_blocks, 1, GATE_BLOCK),
                               b_gate_x[l].reshape(n_gate_blocks, 1, GATE_BLOCK)], axis=-1)
        y_rec = _lru(xr.reshape(seq, bsz, d_rec), yr.reshape(seq, bsz, d_rec), conv_w[l], row(conv_b[l]),
                     wax, bax, row(lru_logit[l]))
        y_att = _attn(q, k, v, cum)

        x1, h2 = _outproj(y_rec.reshape(seq, bsz * d_rec), y_att, x2, w_out[l].astype(BF16),
                          row(g_out_rec[l]), row(g_out_att[l]), gt1, row(g_ffn[l]), sc2, sh2, seq=seq)
        a = _up(h2, w_up[l].astype(BF16), d_ff)
        x2 = _down(a, w_down[l].astype(BF16), x1, gt2, seq=seq)
    return x2.reshape(bsz, seq, d)
```

```python
import functools
import math

import jax
import jax.numpy as jnp
from jax import lax
from jax.experimental import pallas as pl
from jax.experimental.pallas import tpu as pltpu

F32 = jnp.float32
BF16 = jnp.bfloat16

EPS = 1e-6
LRU_C = 8.0
CONV_W = 4
HEAD_DIM = 128
GATE_BLOCK = 128
LOG2E = math.log2(math.e)

LANES = 128
SUBLANES = 8
VMEM_LIMIT_BYTES = 56 * 1024 * 1024

MOD_TN = 1024
TOKEN_TS = 32
LRU_TC = 128
LRU_CB = 256
LRU_TS = 16
ATTN_T = 256
UP_TM, UP_TN = 1024, 512
DOWN_TM, DOWN_TN = 1024, 512


def _params(n_axes):
    return pltpu.CompilerParams(
        dimension_semantics=("arbitrary",) * n_axes,
        vmem_limit_bytes=VMEM_LIMIT_BYTES,
    )


def _dot(a, b):
    return jnp.dot(a, b, preferred_element_type=F32)


def _rms_scale(x):
    return lax.rsqrt(jnp.mean(x * x, axis=-1, keepdims=True) + EPS)


def _mod_kernel(c_ref, w_ref, b_ref, o_ref):
    c = c_ref[...]
    ca = (c * jax.nn.sigmoid(c)).astype(BF16)
    o_ref[...] = _dot(ca, w_ref[...].astype(BF16)) + b_ref[...]


def _mod(c, w_ada, b_ada):
    bsz, d = c.shape
    n = w_ada.shape[1]
    return pl.pallas_call(
        _mod_kernel,
        grid=(n // MOD_TN,),
        in_specs=[
            pl.BlockSpec((bsz, d), lambda j: (0, 0)),
            pl.BlockSpec((d, MOD_TN), lambda j: (0, j)),
            pl.BlockSpec((1, MOD_TN), lambda j: (0, j)),
        ],
        out_specs=pl.BlockSpec((bsz, MOD_TN), lambda j: (0, j)),
        out_shape=jax.ShapeDtypeStruct((bsz, n), F32),
        compiler_params=_params(1),
        name="mod",
    )(c, w_ada, b_ada.reshape(1, n))


def _store_time_major(dst_ref, val, *, bsz, ts):
    for cs in range(dst_ref.shape[0]):
        slab = dst_ref.at[cs]
        for b in range(bsz):
            slab[pl.ds(b, ts, stride=bsz), :] = val[b * ts:(b + 1) * ts, cs * LANES:(cs + 1) * LANES]


def _inproj_kernel(x_ref, g_ref, sc_ref, sh_ref, w_ref, wf_ref, gq_ref, gk_ref,
                   xr_ref, yr_ref, q_ref, k_ref, v_ref, fl_ref, *, d_rec, d_att, n_heads):
    bsz, ts, d = x_ref.shape
    x = x_ref[...]
    y = x * _rms_scale(x) * g_ref[...]
    h = (y * (1.0 + sc_ref[...]) + sh_ref[...]).reshape(bsz * ts, d).astype(BF16)

    _store_time_major(xr_ref, _dot(h, w_ref[:, 0:d_rec]), bsz=bsz, ts=ts)
    _store_time_major(yr_ref, _dot(h, w_ref[:, d_rec:2 * d_rec]), bsz=bsz, ts=ts)
    fl_ref[...] = _dot(h, wf_ref[...]).reshape(bsz, ts, LANES)

    off = 2 * d_rec
    q = _dot(h, w_ref[:, off:off + d_att])
    k = _dot(h, w_ref[:, off + d_att:off + 2 * d_att])
    v = _dot(h, w_ref[:, off + 2 * d_att:off + 3 * d_att])
    q_gain = gq_ref[...] * (HEAD_DIM ** -0.5 * LOG2E)
    k_gain = gk_ref[...]
    for hd in range(n_heads):
        sl = slice(hd * HEAD_DIM, (hd + 1) * HEAD_DIM)
        qh = q[:, sl]
        kh = k[:, sl]
        q_ref[:, hd] = (qh * _rms_scale(qh) * q_gain).astype(BF16).reshape(bsz, ts, HEAD_DIM)
        k_ref[:, hd] = (kh * _rms_scale(kh) * k_gain).astype(BF16).reshape(bsz, ts, HEAD_DIM)
        v_ref[:, hd] = v[:, sl].astype(BF16).reshape(bsz, ts, HEAD_DIM)


def _inproj(x, g_mix, sc1, sh1, w_main, w_f, g_q, g_k, *, d_rec, d_att, n_heads):
    bsz, seq, d = x.shape
    ts = TOKEN_TS
    const = lambda i: (0, 0)
    const3 = lambda i: (0, 0, 0)
    tchunk = lambda i: (0, i, 0)
    slab_shape = jax.ShapeDtypeStruct((d_rec // LANES, seq * bsz, LANES), F32)
    slab_spec = pl.BlockSpec((d_rec // LANES, ts * bsz, LANES), tchunk)
    head_shape = jax.ShapeDtypeStruct((bsz, n_heads, seq, HEAD_DIM), BF16)
    head_spec = pl.BlockSpec((bsz, n_heads, ts, HEAD_DIM), lambda i: (0, 0, i, 0))
    return pl.pallas_call(
        functools.partial(_inproj_kernel, d_rec=d_rec, d_att=d_att, n_heads=n_heads),
        grid=(seq // ts,),
        in_specs=[
            pl.BlockSpec((bsz, ts, d), tchunk),
            pl.BlockSpec((1, d), const),
            pl.BlockSpec((bsz, 1, d), const3),
            pl.BlockSpec((bsz, 1, d), const3),
            pl.BlockSpec(w_main.shape, const, pipeline_mode=pl.Buffered(1)),
            pl.BlockSpec(w_f.shape, const, pipeline_mode=pl.Buffered(1)),
            pl.BlockSpec((1, HEAD_DIM), const),
            pl.BlockSpec((1, HEAD_DIM), const),
        ],
        out_specs=[
            slab_spec, slab_spec,
            head_spec, head_spec, head_spec,
            pl.BlockSpec((bsz, ts, LANES), tchunk),
        ],
        out_shape=[
            slab_shape, slab_shape,
            head_shape, head_shape, head_shape,
            jax.ShapeDtypeStruct((bsz, seq, LANES), F32),
        ],
        compiler_params=_params(1),
        name="inproj",
    )(x, g_mix, sc1, sh1, w_main, w_f, g_q, g_k)


def _cum_kernel(f_ref, b_ref, o_ref):
    z = f_ref[...] + b_ref[...]
    x = -(jnp.maximum(-z, 0.0) + jnp.log1p(jnp.exp(-jnp.abs(z))))
    n = x.shape[-1]
    lane = lax.broadcasted_iota(jnp.int32, x.shape, 1)
    d = 1
    while d < n:
        x = x + jnp.where(lane >= d, pltpu.roll(x, d, axis=1), 0.0)
        d *= 2
    o_ref[...] = x


def _cum(f_rows, b_rows):
    r, n = f_rows.shape
    return pl.pallas_call(
        _cum_kernel,
        grid=(1,),
        in_specs=[pl.BlockSpec((r, n), lambda i: (0, 0)),
                  pl.BlockSpec((r, 1), lambda i: (0, 0))],
        out_specs=pl.BlockSpec((r, n), lambda i: (0, 0)),
        out_shape=jax.ShapeDtypeStruct((r, n), F32),
        compiler_params=_params(1),
        name="cum",
    )(f_rows, b_rows)


def _neg_expm1_2x(x, exp_x):
    return jnp.tanh(-x) * (1.0 + exp_x * exp_x)


def _lru_kernel(xr_ref, yr_ref, cw_ref, cb_ref, wax_ref, bax_ref, lam_ref, o_ref,
                xp_ref, h_ref, *, bsz):
    n_slabs, rows, _ = xr_ref.shape
    tc = rows // bsz
    pad = SUBLANES
    hist = CONV_W - 1
    t_idx = pl.program_id(1)

    @pl.when(t_idx == 0)
    def _():
        xp_ref[:, 0:pad * bsz, :] = jnp.zeros((n_slabs, pad * bsz, LANES), F32)
        h_ref[...] = jnp.zeros(h_ref.shape, F32)

    @pl.when(t_idx > 0)
    def _():
        xp_ref[:, (pad - hist) * bsz:pad * bsz, :] = xp_ref[:, (pad + tc - hist) * bsz:(pad + tc) * bsz, :]

    xp_ref[:, pad * bsz:, :] = xr_ref[...]

    lam = lam_ref[...]
    softplus_neg = jnp.maximum(-lam, 0.0) + jnp.log1p(jnp.exp(-jnp.abs(lam)))
    log_a_gain = -LRU_C * softplus_neg
    n_sub = tc // LRU_TS
    sub_rows = LRU_TS * bsz

    def sub_chunk(j, carry):
        r0 = pl.multiple_of(j * sub_rows, sub_rows)
        for nb in range(n_slabs):
            ln = slice(nb * GATE_BLOCK, (nb + 1) * GATE_BLOCK)
            base = r0 + (pad - hist) * bsz
            u = xp_ref[nb, pl.ds(base, sub_rows), :] * cw_ref[0:1, ln]
            for kk in range(1, CONV_W):
                u = u + xp_ref[nb, pl.ds(base + kk * bsz, sub_rows), :] * cw_ref[kk:kk + 1, ln]
            u = u + cb_ref[:, ln]
            g = _dot(u.astype(BF16), wax_ref[nb]) + bax_ref[nb]
            r = jax.nn.sigmoid(g[:, :GATE_BLOCK])
            i = jax.nn.sigmoid(g[:, GATE_BLOCK:])
            log_a = r * log_a_gain[:, ln]
            a = jnp.exp(log_a)
            b_in = jnp.sqrt(_neg_expm1_2x(log_a, a)) * (i * u)
            h = h_ref[:, ln]
            hs = []
            for tt in range(LRU_TS):
                rs = slice(tt * bsz, (tt + 1) * bsz)
                h = a[rs] * h + b_in[rs]
                hs.append(h)
            h_ref[:, ln] = h
            gate = jax.nn.gelu(yr_ref[nb, pl.ds(r0, sub_rows), :])
            o_ref[nb, pl.ds(r0, sub_rows), :] = jnp.concatenate(hs, axis=0) * gate
        return carry

    lax.fori_loop(0, n_sub, sub_chunk, 0)


def _lru(xr_s, yr_s, conv_w, conv_b, wax, bax, lam, *, bsz):
    n_slabs_total, rows_total, _ = xr_s.shape
    seq = rows_total // bsz
    tc, cb = LRU_TC, LRU_CB
    nblk = cb // GATE_BLOCK
    blk = lambda ci, ti: (ci, ti, 0)
    chan = lambda ci, ti: (0, ci)
    slab_spec = pl.BlockSpec((nblk, tc * bsz, LANES), blk)
    return pl.pallas_call(
        functools.partial(_lru_kernel, bsz=bsz),
        grid=(n_slabs_total // nblk, seq // tc),
        in_specs=[
            slab_spec,
            slab_spec,
            pl.BlockSpec((CONV_W, cb), chan),
            pl.BlockSpec((1, cb), chan),
            pl.BlockSpec((nblk, GATE_BLOCK, 2 * GATE_BLOCK), lambda ci, ti: (ci, 0, 0)),
            pl.BlockSpec((nblk, 1, 2 * GATE_BLOCK), lambda ci, ti: (ci, 0, 0)),
            pl.BlockSpec((1, cb), chan),
        ],
        out_specs=slab_spec,
        out_shape=jax.ShapeDtypeStruct(xr_s.shape, F32),
        scratch_shapes=[
            pltpu.VMEM((nblk, (tc + SUBLANES) * bsz, LANES), F32),
            pltpu.VMEM((bsz, cb), F32),
        ],
        compiler_params=_params(2),
        name="lru",
    )(xr_s, yr_s, conv_w, conv_b, wax, bax, lam)


def _attn_kernel(q_ref, k_ref, v_ref, cum_ref, o_ref):
    seq = q_ref.shape[2]
    t = ATTN_T
    nq = seq // t
    row = lax.broadcasted_iota(jnp.int32, (t, t), 0)
    col = lax.broadcasted_iota(jnp.int32, (t, t), 1)
    causal = col <= row

    state = [None] * nq
    for kj in range(nq):
        kb = k_ref[0, 0, kj * t:(kj + 1) * t, :]
        vb = v_ref[0, 0, kj * t:(kj + 1) * t, :]
        cum_k = cum_ref[0, 0, kj]
        for qi in range(kj, nq):
            q = q_ref[0, 0, qi * t:(qi + 1) * t, :]
            c0 = cum_ref[0, 0, qi][:, 0:1]
            s = lax.dot_general(q, kb, (((1,), (1,)), ((), ())),
                                preferred_element_type=F32) + (c0 - cum_k) * LOG2E
            if qi == kj:
                s = jnp.where(causal, s, -jnp.inf)
            m_blk = jnp.max(s, axis=1, keepdims=True)
            if state[qi] is None:
                m_new = m_blk
                p = jnp.exp2(s - m_new)
                l_new = jnp.sum(p, axis=1, keepdims=True)
                acc_new = _dot(p.astype(BF16), vb)
            else:
                m, l, acc = state[qi]
                m_new = jnp.maximum(m, m_blk)
                alpha = jnp.exp2(m - m_new)
                p = jnp.exp2(s - m_new)
                l_new = alpha * l + jnp.sum(p, axis=1, keepdims=True)
                acc_new = alpha * acc + _dot(p.astype(BF16), vb)
            state[qi] = (m_new, l_new, acc_new)
            if qi == kj:
                o_ref[0, 0, qi * t:(qi + 1) * t, :] = (acc_new / l_new).astype(BF16)


def _attn(q, k, v, cum5):
    bsz, n_heads, seq, dh = q.shape
    spec = pl.BlockSpec((1, 1, seq, dh), lambda b, h: (b, h, 0, 0))
    return pl.pallas_call(
        _attn_kernel,
        grid=(bsz, n_heads),
        in_specs=[spec, spec, spec,
                  pl.BlockSpec((1, 1) + cum5.shape[2:], lambda b, h: (b, h, 0, 0, 0))],
        out_specs=spec,
        out_shape=jax.ShapeDtypeStruct(q.shape, BF16),
        compiler_params=_params(2),
        name="attn",
    )(q, k, v, cum5)


def _outproj_kernel(yrec_ref, yatt_ref, x_ref, w_ref, gor_ref, goa_ref, gt_ref,
                    gf_ref, sc_ref, sh_ref, x1_ref, h2_ref, yr_scr, *, n_heads):
    bsz, ts, d = x_ref.shape
    for cs in range(yrec_ref.shape[0]):
        slab = yrec_ref.at[cs]
        for b in range(bsz):
            yr_scr[b * ts:(b + 1) * ts, cs * LANES:(cs + 1) * LANES] = slab[pl.ds(b, ts, stride=bsz), :]
    yr = yr_scr[...]
    parts = [(yr * _rms_scale(yr) * gor_ref[...]).astype(BF16)]
    heads = [yatt_ref[:, hd].astype(F32).reshape(bsz * ts, HEAD_DIM) for hd in range(n_heads)]
    ssq = heads[0] * heads[0]
    for hh in heads[1:]:
        ssq = ssq + hh * hh
    inv = lax.rsqrt(jnp.sum(ssq, axis=-1, keepdims=True) / (n_heads * HEAD_DIM) + EPS)
    for hd in range(n_heads):
        gain = goa_ref[:, hd * HEAD_DIM:(hd + 1) * HEAD_DIM]
        parts.append((heads[hd] * inv * gain).astype(BF16))
    mix = jnp.concatenate(parts, axis=1)
    x1 = x_ref[...] + gt_ref[...] * _dot(mix, w_ref[...]).reshape(bsz, ts, d)
    x1_ref[...] = x1
    y = x1 * _rms_scale(x1) * gf_ref[...]
    h2_ref[...] = (y * (1.0 + sc_ref[...]) + sh_ref[...]).astype(BF16)


def _outproj(y_rec_s, y_att, x, w_out, g_or, g_oa, gt1, g_ffn, sc2, sh2):
    bsz, seq, d = x.shape
    _, n_heads, _, dh = y_att.shape
    n_slabs = y_rec_s.shape[0]
    ts = TOKEN_TS
    const = lambda i: (0, 0)
    const3 = lambda i: (0, 0, 0)
    tchunk = lambda i: (0, i, 0)
    tok_spec = pl.BlockSpec((bsz, ts, d), tchunk)
    mod_spec = pl.BlockSpec((bsz, 1, d), const3)
    return pl.pallas_call(
        functools.partial(_outproj_kernel, n_heads=n_heads),
        grid=(seq // ts,),
        in_specs=[
            pl.BlockSpec((n_slabs, ts * bsz, LANES), tchunk),
            pl.BlockSpec((bsz, n_heads, ts, dh), lambda i: (0, 0, i, 0)),
            tok_spec,
            pl.BlockSpec(w_out.shape, const, pipeline_mode=pl.Buffered(1)),
            pl.BlockSpec((1, n_slabs * LANES), const),
            pl.BlockSpec((1, n_heads * dh), const),
            mod_spec,
            pl.BlockSpec((1, d), const),
            mod_spec,
            mod_spec,
        ],
        out_specs=[tok_spec, tok_spec],
        out_shape=[jax.ShapeDtypeStruct(x.shape, F32), jax.ShapeDtypeStruct(x.shape, BF16)],
        scratch_shapes=[pltpu.VMEM((bsz * ts, n_slabs * LANES), F32)],
        compiler_params=_params(1),
        name="outproj",
    )(y_rec_s, y_att, x, w_out, g_or, g_oa, gt1, g_ffn, sc2, sh2)


def _up_kernel(h_ref, wg_ref, wu_ref, a_ref):
    h = h_ref[...]
    g = _dot(h, wg_ref[...])
    u = _dot(h, wu_ref[...])
    a_ref[...] = (g * jax.nn.sigmoid(g) * u).astype(BF16)


def _up(h2, w_up, d_ff):
    t, d = h2.shape
    tm, tn = UP_TM, UP_TN
    nj = d_ff // tn
    return pl.pallas_call(
        _up_kernel,
        grid=(t // tm, nj),
        in_specs=[
            pl.BlockSpec((tm, d), lambda i, j: (i, 0)),
            pl.BlockSpec((d, tn), lambda i, j: (0, j)),
            pl.BlockSpec((d, tn), lambda i, j: (0, j + nj)),
        ],
        out_specs=pl.BlockSpec((tm, tn), lambda i, j: (i, j)),
        out_shape=jax.ShapeDtypeStruct((t, d_ff), BF16),
        compiler_params=_params(2),
        name="up",
    )(h2, w_up, w_up)


def _down_kernel(a_ref, w_ref, x1_ref, gt_ref, o_ref):
    o_ref[...] = x1_ref[...] + gt_ref[0] * _dot(a_ref[...], w_ref[...])


def _down(a, w_down, x1, gt2, *, seq):
    t, d_ff = a.shape
    d = w_down.shape[1]
    tm, tn = DOWN_TM, DOWN_TN
    spb = seq // tm
    return pl.pallas_call(
        _down_kernel,
        grid=(t // tm, d // tn),
        in_specs=[
            pl.BlockSpec((tm, d_ff), lambda i, j: (i, 0)),
            pl.BlockSpec((d_ff, tn), lambda i, j: (0, j)),
            pl.BlockSpec((tm, tn), lambda i, j: (i, j)),
            pl.BlockSpec((1, 1, tn), lambda i, j: (i // spb, 0, j)),
        ],
        out_specs=pl.BlockSpec((tm, tn), lambda i, j: (i, j)),
        out_shape=jax.ShapeDtypeStruct((t, d), F32),
        compiler_params=_params(2),
        name="down",
    )(a, w_down, x1, gt2)


def kernel(x, c, w_ada, b_ada, g_mix, w_in, conv_w, conv_b, w_gate_a, b_gate_a, w_gate_x, b_gate_x,
           lru_logit, b_forget, g_q, g_k, g_out_rec, g_out_att, w_out, g_ffn, w_up, w_down):
    bsz, seq, d = x.shape
    depth = w_ada.shape[0]
    d_rec = conv_w.shape[-1]
    n_heads = b_forget.shape[-1]
    d_att = n_heads * HEAD_DIM
    n_gate_blocks = w_gate_a.shape[1]
    d_ff = w_down.shape[1]
    t = bsz * seq
    row = lambda v: v.reshape(1, -1)

    for l in range(depth):
        mod = _mod(c, w_ada[l], b_ada[l])
        sh1, sc1, gt1, sh2, sc2, gt2 = [m.reshape(bsz, 1, d) for m in jnp.split(mod, 6, axis=-1)]

        w_main = w_in[l][:, :2 * d_rec + 3 * d_att].astype(BF16)
        w_f = jnp.pad(w_in[l][:, 2 * d_rec + 3 * d_att:], ((0, 0), (0, LANES - n_heads))).astype(BF16)
        xr, yr, q, k, v, fl = _inproj(x, row(g_mix[l]), sc1, sh1, w_main, w_f, row(g_q[l]), row(g_k[l]),
                                      d_rec=d_rec, d_att=d_att, n_heads=n_heads)

        f_rows = fl[:, :, :n_heads].transpose(0, 2, 1).reshape(bsz * n_heads, seq)
        b_rows = jnp.tile(b_forget[l], bsz).reshape(bsz * n_heads, 1)
        cum = _cum(f_rows, b_rows).reshape(bsz, n_heads, seq // ATTN_T, 1, ATTN_T)

        wax = jnp.concatenate([w_gate_a[l], w_gate_x[l]], axis=-1).astype(BF16)
        bax = jnp.concatenate([b_gate_a[l].reshape(n_gate_blocks, 1, GATE_BLOCK),
                               b_gate_x[l].reshape(n_gate_blocks, 1, GATE_BLOCK)], axis=-1)
        y_rec = _lru(xr, yr, conv_w[l], row(conv_b[l]), wax, bax, row(lru_logit[l]), bsz=bsz)
        y_att = _attn(q, k, v, cum)

        x1, h2 = _outproj(y_rec, y_att, x, w_out[l].astype(BF16), row(g_out_rec[l]), row(g_out_att[l]),
                          gt1, row(g_ffn[l]), sc2, sh2)
        a = _up(h2.reshape(t, d), w_up[l].astype(BF16), d_ff)
        x = _down(a, w_down[l].astype(BF16), x1.reshape(t, d), gt2, seq=seq).reshape(bsz, seq, d)
    return x
```

```python
import functools
import math

import jax
import jax.numpy as jnp
from jax import lax
from jax.experimental import pallas as pl
from jax.experimental.pallas import tpu as pltpu

F32 = jnp.float32
BF16 = jnp.bfloat16

EPS = 1e-6
LRU_C = 8.0
CONV_W = 4
HEAD_DIM = 128
GATE_BLOCK = 128
LOG2E = math.log2(math.e)

LANES = 128
SUBLANES = 8
VMEM_LIMIT_BYTES = 56 * 1024 * 1024

MOD_TN = 1024
TOKEN_TS = 32
LRU_TC = 128
LRU_CB = 512
LRU_TS = 16
ATTN_T = 256
UP_TM, UP_TN = 1024, 512
DOWN_TM, DOWN_TN = 1024, 512


def _params(n_axes):
    return pltpu.CompilerParams(
        dimension_semantics=("arbitrary",) * n_axes,
        vmem_limit_bytes=VMEM_LIMIT_BYTES,
    )


def _dot(a, b):
    return jnp.dot(a, b, preferred_element_type=F32)


def _rms_scale(x):
    return lax.rsqrt(jnp.mean(x * x, axis=-1, keepdims=True) + EPS)


def _mod_kernel(c_ref, w_ref, b_ref, o_ref):
    c = c_ref[...]
    ca = (c * jax.nn.sigmoid(c)).astype(BF16)
    o_ref[...] = _dot(ca, w_ref[...].astype(BF16)) + b_ref[...]


def _mod(c, w_ada, b_ada):
    bsz, d = c.shape
    n = w_ada.shape[1]
    return pl.pallas_call(
        _mod_kernel,
        grid=(n // MOD_TN,),
        in_specs=[
            pl.BlockSpec((bsz, d), lambda j: (0, 0)),
            pl.BlockSpec((d, MOD_TN), lambda j: (0, j)),
            pl.BlockSpec((1, MOD_TN), lambda j: (0, j)),
        ],
        out_specs=pl.BlockSpec((bsz, MOD_TN), lambda j: (0, j)),
        out_shape=jax.ShapeDtypeStruct((bsz, n), F32),
        compiler_params=_params(1),
        name="mod",
    )(c, w_ada, b_ada.reshape(1, n))


def _store_time_major(dst_ref, val, *, bsz, ts):
    for cs in range(dst_ref.shape[0]):
        slab = dst_ref.at[cs]
        for b in range(bsz):
            slab[pl.ds(b, ts, stride=bsz), :] = val[b * ts:(b + 1) * ts, cs * LANES:(cs + 1) * LANES]


def _inproj_kernel(x_ref, g_ref, sc_ref, sh_ref, w_ref, wf_ref, gq_ref, gk_ref,
                   xr_ref, yr_ref, q_ref, k_ref, v_ref, fl_ref, *, d_rec, d_att, n_heads):
    bsz, ts, d = x_ref.shape
    x = x_ref[...]
    y = x * _rms_scale(x) * g_ref[...]
    h = (y * (1.0 + sc_ref[...]) + sh_ref[...]).reshape(bsz * ts, d).astype(BF16)

    _store_time_major(xr_ref, _dot(h, w_ref[:, 0:d_rec]), bsz=bsz, ts=ts)
    _store_time_major(yr_ref, _dot(h, w_ref[:, d_rec:2 * d_rec]), bsz=bsz, ts=ts)
    fl_ref[...] = _dot(h, wf_ref[...])[:, :n_heads].reshape(bsz, ts, n_heads)

    off = 2 * d_rec
    q = _dot(h, w_ref[:, off:off + d_att])
    k = _dot(h, w_ref[:, off + d_att:off + 2 * d_att])
    v = _dot(h, w_ref[:, off + 2 * d_att:off + 3 * d_att])
    q_gain = gq_ref[...] * (HEAD_DIM ** -0.5 * LOG2E)
    k_gain = gk_ref[...]
    for hd in range(n_heads):
        sl = slice(hd * HEAD_DIM, (hd + 1) * HEAD_DIM)
        qh = q[:, sl]
        kh = k[:, sl]
        q_ref[:, hd] = (qh * _rms_scale(qh) * q_gain).astype(BF16).reshape(bsz, ts, HEAD_DIM)
        k_ref[:, hd] = (kh * _rms_scale(kh) * k_gain).astype(BF16).reshape(bsz, ts, HEAD_DIM)
        v_ref[:, hd] = v[:, sl].astype(BF16).reshape(bsz, ts, HEAD_DIM)


def _inproj(x, g_mix, sc1, sh1, w_main, w_f, g_q, g_k, *, d_rec, d_att, n_heads):
    bsz, seq, d = x.shape
    ts = TOKEN_TS
    const = lambda i: (0, 0)
    const3 = lambda i: (0, 0, 0)
    tchunk = lambda i: (0, i, 0)
    slab_shape = jax.ShapeDtypeStruct((d_rec // LANES, seq * bsz, LANES), F32)
    slab_spec = pl.BlockSpec((d_rec // LANES, ts * bsz, LANES), tchunk)
    head_shape = jax.ShapeDtypeStruct((bsz, n_heads, seq, HEAD_DIM), BF16)
    head_spec = pl.BlockSpec((bsz, n_heads, ts, HEAD_DIM), lambda i: (0, 0, i, 0))
    return pl.pallas_call(
        functools.partial(_inproj_kernel, d_rec=d_rec, d_att=d_att, n_heads=n_heads),
        grid=(seq // ts,),
        in_specs=[
            pl.BlockSpec((bsz, ts, d), tchunk),
            pl.BlockSpec((1, d), const),
            pl.BlockSpec((bsz, 1, d), const3),
            pl.BlockSpec((bsz, 1, d), const3),
            pl.BlockSpec((d, 2 * d_rec + 3 * d_att), const, pipeline_mode=pl.Buffered(1)),
            pl.BlockSpec(w_f.shape, const, pipeline_mode=pl.Buffered(1)),
            pl.BlockSpec((1, HEAD_DIM), const),
            pl.BlockSpec((1, HEAD_DIM), const),
        ],
        out_specs=[
            slab_spec, slab_spec,
            head_spec, head_spec, head_spec,
            pl.BlockSpec((bsz, ts, n_heads), tchunk),
        ],
        out_shape=[
            slab_shape, slab_shape,
            head_shape, head_shape, head_shape,
            jax.ShapeDtypeStruct((bsz, seq, n_heads), F32),
        ],
        compiler_params=_params(1),
        name="inproj",
    )(x, g_mix, sc1, sh1, w_main, w_f, g_q, g_k)


def _cum_kernel(f_ref, b_ref, o_ref):
    z = f_ref[...] + b_ref[...]
    x = -(jnp.maximum(-z, 0.0) + jnp.log1p(jnp.exp(-jnp.abs(z))))
    n = x.shape[-1]
    lane = lax.broadcasted_iota(jnp.int32, x.shape, 1)
    d = 1
    while d < n:
        x = x + jnp.where(lane >= d, pltpu.roll(x, d, axis=1), 0.0)
        d *= 2
    o_ref[...] = x


def _cum(f_rows, b_rows):
    r, n = f_rows.shape
    return pl.pallas_call(
        _cum_kernel,
        grid=(1,),
        in_specs=[pl.BlockSpec((r, n), lambda i: (0, 0)),
                  pl.BlockSpec((r, 1), lambda i: (0, 0))],
        out_specs=pl.BlockSpec((r, n), lambda i: (0, 0)),
        out_shape=jax.ShapeDtypeStruct((r, n), F32),
        compiler_params=_params(1),
        name="cum",
    )(f_rows, b_rows)


def _sigmoid(x):
    return 1.0 / (1.0 + jnp.exp2(x * -LOG2E))


def _gelu_tanh(y):
    c1 = -2.0 * math.sqrt(2.0 / math.pi) * LOG2E
    c3 = c1 * 0.044715
    return y / (1.0 + jnp.exp2(y * (c1 + c3 * (y * y))))


def _lru_kernel(xr_ref, yr_ref, cw_ref, cb_ref, wax_ref, bax_ref, lam_ref, o_ref,
                xp_ref, h_ref, *, bsz):
    n_slabs, rows, _ = xr_ref.shape
    tc = rows // bsz
    pad = SUBLANES
    hist = CONV_W - 1
    t_idx = pl.program_id(1)

    @pl.when(t_idx == 0)
    def _():
        xp_ref[:, 0:pad * bsz, :] = jnp.zeros((n_slabs, pad * bsz, LANES), F32)
        h_ref[...] = jnp.zeros(h_ref.shape, F32)

    @pl.when(t_idx > 0)
    def _():
        xp_ref[:, (pad - hist) * bsz:pad * bsz, :] = xp_ref[:, (pad + tc - hist) * bsz:(pad + tc) * bsz, :]

    xp_ref[:, pad * bsz:, :] = xr_ref[...]

    lam = lam_ref[...]
    softplus_neg = jnp.maximum(-lam, 0.0) + jnp.log1p(jnp.exp(-jnp.abs(lam)))
    neg_log_a_gain = LRU_C * softplus_neg
    log2_a_gain = neg_log_a_gain * -LOG2E
    n_sub = tc // LRU_TS
    sub_rows = LRU_TS * bsz

    def sub_chunk(j, carry):
        r0 = pl.multiple_of(j * sub_rows, sub_rows)
        for nb in range(n_slabs):
            ln = slice(nb * GATE_BLOCK, (nb + 1) * GATE_BLOCK)
            base = r0 + (pad - hist) * bsz
            u = xp_ref[nb, pl.ds(base, sub_rows), :] * cw_ref[0:1, ln]
            for kk in range(1, CONV_W):
                u = u + xp_ref[nb, pl.ds(base + kk * bsz, sub_rows), :] * cw_ref[kk:kk + 1, ln]
            u = u + cb_ref[:, ln]
            g = _dot(u.astype(BF16), wax_ref[nb]) + bax_ref[nb]
            r = _sigmoid(g[:, :GATE_BLOCK])
            i = _sigmoid(g[:, GATE_BLOCK:])
            a = jnp.exp2(r * log2_a_gain[:, ln])
            one_minus_a2 = jnp.tanh(r * neg_log_a_gain[:, ln]) * (1.0 + a * a)
            mult = jnp.where(one_minus_a2 > 0.0, one_minus_a2 * lax.rsqrt(one_minus_a2), 0.0)
            b_in = mult * (i * u)
            h = h_ref[:, ln]
            hs = []
            for tt in range(LRU_TS):
                rs = slice(tt * bsz, (tt + 1) * bsz)
                h = a[rs] * h + b_in[rs]
                hs.append(h)
            h_ref[:, ln] = h
            gate = _gelu_tanh(yr_ref[nb, pl.ds(r0, sub_rows), :])
            o_ref[nb, pl.ds(r0, sub_rows), :] = jnp.concatenate(hs, axis=0) * gate
        return carry

    lax.fori_loop(0, n_sub, sub_chunk, 0)


def _lru(xr_s, yr_s, conv_w, conv_b, wax, bax, lam, *, bsz):
    n_slabs_total, rows_total, _ = xr_s.shape
    seq = rows_total // bsz
    tc, cb = LRU_TC, LRU_CB
    nblk = cb // GATE_BLOCK
    blk = lambda ci, ti: (ci, ti, 0)
    chan = lambda ci, ti: (0, ci)
    slab_spec = pl.BlockSpec((nblk, tc * bsz, LANES), blk)
    return pl.pallas_call(
        functools.partial(_lru_kernel, bsz=bsz),
        grid=(n_slabs_total // nblk, seq // tc),
        in_specs=[
            slab_spec,
            slab_spec,
            pl.BlockSpec((CONV_W, cb), chan),
            pl.BlockSpec((1, cb), chan),
            pl.BlockSpec((nblk, GATE_BLOCK, 2 * GATE_BLOCK), lambda ci, ti: (ci, 0, 0)),
            pl.BlockSpec((nblk, 1, 2 * GATE_BLOCK), lambda ci, ti: (ci, 0, 0)),
            pl.BlockSpec((1, cb), chan),
        ],
        out_specs=slab_spec,
        out_shape=jax.ShapeDtypeStruct(xr_s.shape, F32),
        scratch_shapes=[
            pltpu.VMEM((nblk, (tc + SUBLANES) * bsz, LANES), F32),
            pltpu.VMEM((bsz, cb), F32),
        ],
        compiler_params=_params(2),
        name="lru",
    )(xr_s, yr_s, conv_w, conv_b, wax, bax, lam)


def _attn_kernel(q_ref, k_ref, v_ref, cum_ref, o_ref):
    seq = q_ref.shape[2]
    t = ATTN_T
    nq = seq // t
    row = lax.broadcasted_iota(jnp.int32, (t, t), 0)
    col = lax.broadcasted_iota(jnp.int32, (t, t), 1)
    causal = col <= row

    state = [None] * nq
    for kj in range(nq):
        kb = k_ref[0, 0, kj * t:(kj + 1) * t, :]
        vb = v_ref[0, 0, kj * t:(kj + 1) * t, :]
        cum_k = cum_ref[0, 0, kj]
        for qi in range(kj, nq):
            q = q_ref[0, 0, qi * t:(qi + 1) * t, :]
            c0 = cum_ref[0, 0, qi][:, 0:1]
            s = lax.dot_general(q, kb, (((1,), (1,)), ((), ())),
                                preferred_element_type=F32) + (c0 - cum_k) * LOG2E
            if qi == kj:
                s = jnp.where(causal, s, -jnp.inf)
            m_blk = jnp.max(s, axis=1, keepdims=True)
            if state[qi] is None:
                m_new = m_blk
                p = jnp.exp2(s - m_new)
                l_new = jnp.sum(p, axis=1, keepdims=True)
                acc_new = _dot(p.astype(BF16), vb)
            else:
                m, l, acc = state[qi]
                m_new = jnp.maximum(m, m_blk)
                alpha = jnp.exp2(m - m_new)
                p = jnp.exp2(s - m_new)
                l_new = alpha * l + jnp.sum(p, axis=1, keepdims=True)
                acc_new = alpha * acc + _dot(p.astype(BF16), vb)
            state[qi] = (m_new, l_new, acc_new)
            if qi == kj:
                o_ref[0, 0, qi * t:(qi + 1) * t, :] = (acc_new / l_new).astype(BF16)


def _attn(q, k, v, cum5):
    bsz, n_heads, seq, dh = q.shape
    spec = pl.BlockSpec((1, 1, seq, dh), lambda b, h: (b, h, 0, 0))
    return pl.pallas_call(
        _attn_kernel,
        grid=(bsz, n_heads),
        in_specs=[spec, spec, spec,
                  pl.BlockSpec((1, 1) + cum5.shape[2:], lambda b, h: (b, h, 0, 0, 0))],
        out_specs=spec,
        out_shape=jax.ShapeDtypeStruct(q.shape, BF16),
        compiler_params=_params(2),
        name="attn",
    )(q, k, v, cum5)


def _outproj_kernel(yrec_ref, yatt_ref, x_ref, w_ref, gor_ref, goa_ref, gt_ref,
                    gf_ref, sc_ref, sh_ref, x1_ref, h2_ref, yr_scr, *, n_heads):
    bsz, ts, d = x_ref.shape
    for cs in range(yrec_ref.shape[0]):
        slab = yrec_ref.at[cs]
        for b in range(bsz):
            yr_scr[b * ts:(b + 1) * ts, cs * LANES:(cs + 1) * LANES] = slab[pl.ds(b, ts, stride=bsz), :]
    yr = yr_scr[...]
    parts = [(yr * _rms_scale(yr) * gor_ref[...]).astype(BF16)]
    heads = [yatt_ref[:, hd].astype(F32).reshape(bsz * ts, HEAD_DIM) for hd in range(n_heads)]
    ssq = heads[0] * heads[0]
    for hh in heads[1:]:
        ssq = ssq + hh * hh
    inv = lax.rsqrt(jnp.sum(ssq, axis=-1, keepdims=True) / (n_heads * HEAD_DIM) + EPS)
    for hd in range(n_heads):
        gain = goa_ref[:, hd * HEAD_DIM:(hd + 1) * HEAD_DIM]
        parts.append((heads[hd] * inv * gain).astype(BF16))
    mix = jnp.concatenate(parts, axis=1)
    x1 = x_ref[...] + gt_ref[...] * _dot(mix, w_ref[...]).reshape(bsz, ts, d)
    x1_ref[...] = x1
    y = x1 * _rms_scale(x1) * gf_ref[...]
    h2_ref[...] = (y * (1.0 + sc_ref[...]) + sh_ref[...]).astype(BF16)


def _outproj(y_rec_s, y_att, x, w_out, g_or, g_oa, gt1, g_ffn, sc2, sh2):
    bsz, seq, d = x.shape
    _, n_heads, _, dh = y_att.shape
    n_slabs = y_rec_s.shape[0]
    ts = TOKEN_TS
    const = lambda i: (0, 0)
    const3 = lambda i: (0, 0, 0)
    tchunk = lambda i: (0, i, 0)
    tok_spec = pl.BlockSpec((bsz, ts, d), tchunk)
    mod_spec = pl.BlockSpec((bsz, 1, d), const3)
    return pl.pallas_call(
        functools.partial(_outproj_kernel, n_heads=n_heads),
        grid=(seq // ts,),
        in_specs=[
            pl.BlockSpec((n_slabs, ts * bsz, LANES), tchunk),
            pl.BlockSpec((bsz, n_heads, ts, dh), lambda i: (0, 0, i, 0)),
            tok_spec,
            pl.BlockSpec(w_out.shape, const, pipeline_mode=pl.Buffered(1)),
            pl.BlockSpec((1, n_slabs * LANES), const),
            pl.BlockSpec((1, n_heads * dh), const),
            mod_spec,
            pl.BlockSpec((1, d), const),
            mod_spec,
            mod_spec,
        ],
        out_specs=[tok_spec, tok_spec],
        out_shape=[jax.ShapeDtypeStruct(x.shape, F32), jax.ShapeDtypeStruct(x.shape, BF16)],
        scratch_shapes=[pltpu.VMEM((bsz * ts, n_slabs * LANES), F32)],
        compiler_params=_params(1),
        name="outproj",
    )(y_rec_s, y_att, x, w_out, g_or, g_oa, gt1, g_ffn, sc2, sh2)


def _up_kernel(h_ref, wg_ref, wu_ref, a_ref, wg_b, wu_b):
    @pl.when(pl.program_id(1) == 0)
    def _():
        wg_b[...] = wg_ref[...].astype(BF16)
        wu_b[...] = wu_ref[...].astype(BF16)

    h = h_ref[...]
    g = _dot(h, wg_b[...])
    u = _dot(h, wu_b[...])
    a_ref[...] = (g * jax.nn.sigmoid(g) * u).astype(BF16)


def _up(h2, w_up, d_ff):
    t, d = h2.shape
    tm, tn = UP_TM, UP_TN
    nj = d_ff // tn
    return pl.pallas_call(
        _up_kernel,
        grid=(nj, t // tm),
        in_specs=[
            pl.BlockSpec((tm, d), lambda j, i: (i, 0)),
            pl.BlockSpec((d, tn), lambda j, i: (0, j)),
            pl.BlockSpec((d, tn), lambda j, i: (0, j + nj)),
        ],
        out_specs=pl.BlockSpec((tm, tn), lambda j, i: (i, j)),
        out_shape=jax.ShapeDtypeStruct((t, d_ff), BF16),
        scratch_shapes=[pltpu.VMEM((d, tn), BF16), pltpu.VMEM((d, tn), BF16)],
        compiler_params=_params(2),
        name="up",
    )(h2, w_up, w_up)


def _down_kernel(a_ref, w_ref, x1_ref, gt_ref, o_ref):
    o_ref[...] = x1_ref[...] + gt_ref[0] * _dot(a_ref[...], w_ref[...])


def _down(a, w_down, x1, gt2, *, seq):
    t, d_ff = a.shape
    d = w_down.shape[1]
    tm, tn = DOWN_TM, DOWN_TN
    spb = seq // tm
    return pl.pallas_call(
        _down_kernel,
        grid=(t // tm, d // tn),
        in_specs=[
            pl.BlockSpec((tm, d_ff), lambda i, j: (i, 0)),
            pl.BlockSpec((d_ff, tn), lambda i, j: (0, j)),
            pl.BlockSpec((tm, tn), lambda i, j: (i, j)),
            pl.BlockSpec((1, 1, tn), lambda i, j: (i // spb, 0, j)),
        ],
        out_specs=pl.BlockSpec((tm, tn), lambda i, j: (i, j)),
        out_shape=jax.ShapeDtypeStruct((t, d), F32),
        compiler_params=_params(2),
        name="down",
    )(a, w_down, x1, gt2)


def kernel(x, c, w_ada, b_ada, g_mix, w_in, conv_w, conv_b, w_gate_a, b_gate_a, w_gate_x, b_gate_x,
           lru_logit, b_forget, g_q, g_k, g_out_rec, g_out_att, w_out, g_ffn, w_up, w_down):
    bsz, seq, d = x.shape
    depth = w_ada.shape[0]
    d_rec = conv_w.shape[-1]
    n_heads = b_forget.shape[-1]
    d_att = n_heads * HEAD_DIM
    n_gate_blocks = w_gate_a.shape[1]
    d_ff = w_down.shape[1]
    t = bsz * seq
    row = lambda v: v.reshape(1, -1)

    for l in range(depth):
        mod = _mod(c, w_ada[l], b_ada[l])
        sh1, sc1, gt1, sh2, sc2, gt2 = [m.reshape(bsz, 1, d) for m in jnp.split(mod, 6, axis=-1)]

        w_in_b = w_in[l].astype(BF16)
        w_f = jnp.pad(w_in_b[:, 2 * d_rec + 3 * d_att:], ((0, 0), (0, LANES - n_heads)))
        xr, yr, q, k, v, fl = _inproj(x, row(g_mix[l]), sc1, sh1, w_in_b, w_f, row(g_q[l]), row(g_k[l]),
                                      d_rec=d_rec, d_att=d_att, n_heads=n_heads)

        f_rows = fl.transpose(0, 2, 1).reshape(bsz * n_heads, seq)
        b_rows = jnp.tile(b_forget[l], bsz).reshape(bsz * n_heads, 1)
        cum = _cum(f_rows, b_rows).reshape(bsz, n_heads, seq // ATTN_T, 1, ATTN_T)

        wax = jnp.concatenate([w_gate_a[l], w_gate_x[l]], axis=-1).astype(BF16)
        bax = jnp.concatenate([b_gate_a[l].reshape(n_gate_blocks, 1, GATE_BLOCK),
                               b_gate_x[l].reshape(n_gate_blocks, 1, GATE_BLOCK)], axis=-1)
        y_rec = _lru(xr, yr, conv_w[l], row(conv_b[l]), wax, bax, row(lru_logit[l]), bsz=bsz)
        y_att = _attn(q, k, v, cum)

        x1, h2 = _outproj(y_rec, y_att, x, w_out[l].astype(BF16), row(g_out_rec[l]), row(g_out_att[l]),
                          gt1, row(g_ffn[l]), sc2, sh2)
        a = _up(h2.reshape(t, d), w_up[l], d_ff)
        x = _down(a, w_down[l].astype(BF16), x1.reshape(t, d), gt2, seq=seq).reshape(bsz, seq, d)
    return x
```

```python
import functools
import math

import jax
import jax.numpy as jnp
from jax import lax
from jax.experimental import pallas as pl
from jax.experimental.pallas import tpu as pltpu

F32 = jnp.float32
BF16 = jnp.bfloat16

EPS = 1e-6
LRU_C = 8.0
CONV_W = 4
HEAD_DIM = 128
GATE_BLOCK = 128
LOG2E = math.log2(math.e)

LANES = 128
SUBLANES = 8
VMEM_LIMIT_BYTES = 56 * 1024 * 1024

MOD_TN = 1024
TOKEN_TS = 32
ATTN_T = 256
UP_TM, UP_TN = 1024, 512
DOWN_TM, DOWN_TN = 1024, 512


def _params(n_axes):
    return pltpu.CompilerParams(
        dimension_semantics=("arbitrary",) * n_axes,
        vmem_limit_bytes=VMEM_LIMIT_BYTES,
    )


def _dot(a, b):
    return jnp.dot(a, b, preferred_element_type=F32)


def _rms_scale(x):
    return lax.rsqrt(jnp.mean(x * x, axis=-1, keepdims=True) + EPS)


def _mod_kernel(c_ref, w_ref, b_ref, o_ref):
    c = c_ref[...]
    ca = (c * jax.nn.sigmoid(c)).astype(BF16)
    o_ref[...] = _dot(ca, w_ref[...].astype(BF16)) + b_ref[...]


def _mod(c, w_ada, b_ada):
    bsz, d = c.shape
    n = w_ada.shape[1]
    return pl.pallas_call(
        _mod_kernel,
        grid=(n // MOD_TN,),
        in_specs=[
            pl.BlockSpec((bsz, d), lambda j: (0, 0)),
            pl.BlockSpec((d, MOD_TN), lambda j: (0, j)),
            pl.BlockSpec((1, MOD_TN), lambda j: (0, j)),
        ],
        out_specs=pl.BlockSpec((bsz, MOD_TN), lambda j: (0, j)),
        out_shape=jax.ShapeDtypeStruct((bsz, n), F32),
        compiler_params=_params(1),
        name="mod",
    )(c, w_ada, b_ada.reshape(1, n))


def _sigmoid(x):
    return 1.0 / (1.0 + jnp.exp2(x * -LOG2E))


def _gelu_tanh(y):
    c1 = -2.0 * math.sqrt(2.0 / math.pi) * LOG2E
    c3 = c1 * 0.044715
    return y / (1.0 + jnp.exp2(y * (c1 + c3 * (y * y))))


def _inproj_kernel(x_ref, g_ref, sc_ref, sh_ref, w_ref, wf_ref, gq_ref, gk_ref,
                   cw_ref, cb_ref, wax_ref, bax_ref, lam_ref,
                   yrec_ref, q_ref, k_ref, v_ref, fl_ref,
                   xs_ref, hs_ref, hstate_ref, *, d_rec, d_att, n_heads):
    bsz, ts, d = x_ref.shape
    rows = bsz * ts
    n_slabs = d_rec // LANES
    pad = SUBLANES
    hist = CONV_W - 1
    step = pl.program_id(0)

    @pl.when(step == 0)
    def _():
        xs_ref[:, 0:pad * bsz, :] = jnp.zeros((n_slabs, pad * bsz, LANES), F32)
        hstate_ref[...] = jnp.zeros(hstate_ref.shape, F32)

    @pl.when(step > 0)
    def _():
        xs_ref[:, (pad - hist) * bsz:pad * bsz, :] = xs_ref[:, (pad + ts - hist) * bsz:(pad + ts) * bsz, :]

    x = x_ref[...]
    y = x * _rms_scale(x) * g_ref[...]
    h = (y * (1.0 + sc_ref[...]) + sh_ref[...]).reshape(rows, d).astype(BF16)

    xr = _dot(h, w_ref[:, 0:d_rec])
    for cs in range(n_slabs):
        slab = xs_ref.at[cs]
        for b in range(bsz):
            slab[pl.ds(pad * bsz + b, ts, stride=bsz), :] = xr[b * ts:(b + 1) * ts, cs * LANES:(cs + 1) * LANES]
    gate = _gelu_tanh(_dot(h, w_ref[:, d_rec:2 * d_rec]))
    fl_ref[...] = _dot(h, wf_ref[...])[:, :n_heads].reshape(bsz, ts, n_heads)

    off = 2 * d_rec
    q = _dot(h, w_ref[:, off:off + d_att])
    k = _dot(h, w_ref[:, off + d_att:off + 2 * d_att])
    v = _dot(h, w_ref[:, off + 2 * d_att:off + 3 * d_att])
    q_gain = gq_ref[...] * (HEAD_DIM ** -0.5 * LOG2E)
    k_gain = gk_ref[...]
    for hd in range(n_heads):
        sl = slice(hd * HEAD_DIM, (hd + 1) * HEAD_DIM)
        qh = q[:, sl]
        kh = k[:, sl]
        q_ref[:, hd] = (qh * _rms_scale(qh) * q_gain).astype(BF16).reshape(bsz, ts, HEAD_DIM)
        k_ref[:, hd] = (kh * _rms_scale(kh) * k_gain).astype(BF16).reshape(bsz, ts, HEAD_DIM)
        v_ref[:, hd] = v[:, sl].astype(BF16).reshape(bsz, ts, HEAD_DIM)

    lam = lam_ref[...]
    softplus_neg = jnp.maximum(-lam, 0.0) + jnp.log1p(jnp.exp(-jnp.abs(lam)))
    neg_log_a_gain = LRU_C * softplus_neg
    log2_a_gain = neg_log_a_gain * -LOG2E
    base = (pad - hist) * bsz
    for cs in range(n_slabs):
        ln = slice(cs * GATE_BLOCK, (cs + 1) * GATE_BLOCK)
        u = xs_ref[cs, base:base + rows, :] * cw_ref[0:1, ln]
        for kk in range(1, CONV_W):
            u = u + xs_ref[cs, base + kk * bsz:base + kk * bsz + rows, :] * cw_ref[kk:kk + 1, ln]
        u = u + cb_ref[:, ln]
        g = _dot(u.astype(BF16), wax_ref[cs]) + bax_ref[cs]
        r = _sigmoid(g[:, :GATE_BLOCK])
        i = _sigmoid(g[:, GATE_BLOCK:])
        a = jnp.exp2(r * log2_a_gain[:, ln])
        one_minus_a2 = jnp.tanh(r * neg_log_a_gain[:, ln]) * (1.0 + a * a)
        mult = jnp.where(one_minus_a2 > 0.0, one_minus_a2 * lax.rsqrt(one_minus_a2), 0.0)
        b_in = mult * (i * u)
        hcur = hstate_ref[:, ln]
        hseq = []
        for tt in range(ts):
            rs = slice(tt * bsz, (tt + 1) * bsz)
            hcur = a[rs] * hcur + b_in[rs]
            hseq.append(hcur)
        hstate_ref[:, ln] = hcur
        hs_ref[cs] = jnp.concatenate(hseq, axis=0)
    for cs in range(n_slabs):
        ln = slice(cs * LANES, (cs + 1) * LANES)
        slab = hs_ref.at[cs]
        for b in range(bsz):
            hb = slab[pl.ds(b, ts, stride=bsz), :]
            yrec_ref[b, :, ln] = (hb * gate[b * ts:(b + 1) * ts, ln]).astype(BF16)


def _inproj(x, g_mix, sc1, sh1, w_main, w_f, g_q, g_k, conv_w, conv_b, wax, bax, lam,
            *, d_rec, d_att, n_heads):
    bsz, seq, d = x.shape
    ts = TOKEN_TS
    n_slabs = d_rec // LANES
    const = lambda i: (0, 0)
    const3 = lambda i: (0, 0, 0)
    tchunk = lambda i: (0, i, 0)
    head_shape = jax.ShapeDtypeStruct((bsz, n_heads, seq, HEAD_DIM), BF16)
    head_spec = pl.BlockSpec((bsz, n_heads, ts, HEAD_DIM), lambda i: (0, 0, i, 0))
    return pl.pallas_call(
        functools.partial(_inproj_kernel, d_rec=d_rec, d_att=d_att, n_heads=n_heads),
        grid=(seq // ts,),
        in_specs=[
            pl.BlockSpec((bsz, ts, d), tchunk),
            pl.BlockSpec((1, d), const),
            pl.BlockSpec((bsz, 1, d), const3),
            pl.BlockSpec((bsz, 1, d), const3),
            pl.BlockSpec((d, 2 * d_rec + 3 * d_att), const, pipeline_mode=pl.Buffered(1)),
            pl.BlockSpec(w_f.shape, const, pipeline_mode=pl.Buffered(1)),
            pl.BlockSpec((1, HEAD_DIM), const),
            pl.BlockSpec((1, HEAD_DIM), const),
            pl.BlockSpec(conv_w.shape, const),
            pl.BlockSpec(conv_b.shape, const),
            pl.BlockSpec(wax.shape, const3),
            pl.BlockSpec(bax.shape, const3),
            pl.BlockSpec(lam.shape, const),
        ],
        out_specs=[
            pl.BlockSpec((bsz, ts, d_rec), tchunk),
            head_spec, head_spec, head_spec,
            pl.BlockSpec((bsz, ts, n_heads), tchunk),
        ],
        out_shape=[
            jax.ShapeDtypeStruct((bsz, seq, d_rec), BF16),
            head_shape, head_shape, head_shape,
            jax.ShapeDtypeStruct((bsz, seq, n_heads), F32),
        ],
        scratch_shapes=[
            pltpu.VMEM((n_slabs, (SUBLANES + ts) * bsz, LANES), F32),
            pltpu.VMEM((n_slabs, ts * bsz, LANES), F32),
            pltpu.VMEM((bsz, d_rec), F32),
        ],
        compiler_params=_params(1),
        name="inproj",
    )(x, g_mix, sc1, sh1, w_main, w_f, g_q, g_k, conv_w, conv_b, wax, bax, lam)


def _cum_kernel(f_ref, b_ref, o_ref):
    z = f_ref[...] + b_ref[...]
    x = -(jnp.maximum(-z, 0.0) + jnp.log1p(jnp.exp(-jnp.abs(z))))
    n = x.shape[-1]
    lane = lax.broadcasted_iota(jnp.int32, x.shape, 1)
    d = 1
    while d < n:
        x = x + jnp.where(lane >= d, pltpu.roll(x, d, axis=1), 0.0)
        d *= 2
    o_ref[...] = x


def _cum(f_rows, b_rows):
    r, n = f_rows.shape
    return pl.pallas_call(
        _cum_kernel,
        grid=(1,),
        in_specs=[pl.BlockSpec((r, n), lambda i: (0, 0)),
                  pl.BlockSpec((r, 1), lambda i: (0, 0))],
        out_specs=pl.BlockSpec((r, n), lambda i: (0, 0)),
        out_shape=jax.ShapeDtypeStruct((r, n), F32),
        compiler_params=_params(1),
        name="cum",
    )(f_rows, b_rows)


def _attn_kernel(q_ref, k_ref, v_ref, cum_ref, o_ref):
    seq = q_ref.shape[2]
    t = ATTN_T
    nq = seq // t
    row = lax.broadcasted_iota(jnp.int32, (t, t), 0)
    col = lax.broadcasted_iota(jnp.int32, (t, t), 1)
    causal = col <= row

    state = [None] * nq
    for kj in range(nq):
        kb = k_ref[0, 0, kj * t:(kj + 1) * t, :]
        vb = v_ref[0, 0, kj * t:(kj + 1) * t, :]
        cum_k = cum_ref[0, 0, kj]
        for qi in range(kj, nq):
            q = q_ref[0, 0, qi * t:(qi + 1) * t, :]
            c0 = cum_ref[0, 0, qi][:, 0:1]
            s = lax.dot_general(q, kb, (((1,), (1,)), ((), ())),
                                preferred_element_type=F32) + (c0 - cum_k) * LOG2E
            if qi == kj:
                s = jnp.where(causal, s, -jnp.inf)
            m_blk = jnp.max(s, axis=1, keepdims=True)
            if state[qi] is None:
                m_new = m_blk
                p = jnp.exp2(s - m_new)
                l_new = jnp.sum(p, axis=1, keepdims=True)
                acc_new = _dot(p.astype(BF16), vb)
            else:
                m, l, acc = state[qi]
                m_new = jnp.maximum(m, m_blk)
                alpha = jnp.exp2(m - m_new)
                p = jnp.exp2(s - m_new)
                l_new = alpha * l + jnp.sum(p, axis=1, keepdims=True)
                acc_new = alpha * acc + _dot(p.astype(BF16), vb)
            state[qi] = (m_new, l_new, acc_new)
            if qi == kj:
                o_ref[0, 0, qi * t:(qi + 1) * t, :] = (acc_new / l_new).astype(BF16)


def _attn(q, k, v, cum5):
    bsz, n_heads, seq, dh = q.shape
    spec = pl.BlockSpec((1, 1, seq, dh), lambda b, h: (b, h, 0, 0))
    return pl.pallas_call(
        _attn_kernel,
        grid=(bsz, n_heads),
        in_specs=[spec, spec, spec,
                  pl.BlockSpec((1, 1) + cum5.shape[2:], lambda b, h: (b, h, 0, 0, 0))],
        out_specs=spec,
        out_shape=jax.ShapeDtypeStruct(q.shape, BF16),
        compiler_params=_params(2),
        name="attn",
    )(q, k, v, cum5)


def _outproj_kernel(yrec_ref, yatt_ref, x_ref, w_ref, gor_ref, goa_ref, gt_ref,
                    gf_ref, sc_ref, sh_ref, x1_ref, h2_ref, *, n_heads):
    bsz, ts, d = x_ref.shape
    yr = yrec_ref[...].astype(F32).reshape(bsz * ts, yrec_ref.shape[2])
    parts = [(yr * _rms_scale(yr) * gor_ref[...]).astype(BF16)]
    heads = [yatt_ref[:, hd].astype(F32).reshape(bsz * ts, HEAD_DIM) for hd in range(n_heads)]
    ssq = heads[0] * heads[0]
    for hh in heads[1:]:
        ssq = ssq + hh * hh
    inv = lax.rsqrt(jnp.sum(ssq, axis=-1, keepdims=True) / (n_heads * HEAD_DIM) + EPS)
    for hd in range(n_heads):
        gain = goa_ref[:, hd * HEAD_DIM:(hd + 1) * HEAD_DIM]
        parts.append((heads[hd] * inv * gain).astype(BF16))
    mix = jnp.concatenate(parts, axis=1)
    x1 = x_ref[...] + gt_ref[...] * _dot(mix, w_ref[...]).reshape(bsz, ts, d)
    x1_ref[...] = x1
    y = x1 * _rms_scale(x1) * gf_ref[...]
    h2_ref[...] = (y * (1.0 + sc_ref[...]) + sh_ref[...]).astype(BF16)


def _outproj(y_rec, y_att, x, w_out, g_or, g_oa, gt1, g_ffn, sc2, sh2):
    bsz, seq, d = x.shape
    _, n_heads, _, dh = y_att.shape
    d_rec = y_rec.shape[2]
    ts = TOKEN_TS
    const = lambda i: (0, 0)
    const3 = lambda i: (0, 0, 0)
    tchunk = lambda i: (0, i, 0)
    tok_spec = pl.BlockSpec((bsz, ts, d), tchunk)
    mod_spec = pl.BlockSpec((bsz, 1, d), const3)
    return pl.pallas_call(
        functools.partial(_outproj_kernel, n_heads=n_heads),
        grid=(seq // ts,),
        in_specs=[
            pl.BlockSpec((bsz, ts, d_rec), tchunk),
            pl.BlockSpec((bsz, n_heads, ts, dh), lambda i: (0, 0, i, 0)),
            tok_spec,
            pl.BlockSpec(w_out.shape, const, pipeline_mode=pl.Buffered(1)),
            pl.BlockSpec((1, d_rec), const),
            pl.BlockSpec((1, n_heads * dh), const),
            mod_spec,
            pl.BlockSpec((1, d), const),
            mod_spec,
            mod_spec,
        ],
        out_specs=[tok_spec, tok_spec],
        out_shape=[jax.ShapeDtypeStruct(x.shape, F32), jax.ShapeDtypeStruct(x.shape, BF16)],
        compiler_params=_params(1),
        name="outproj",
    )(y_rec, y_att, x, w_out, g_or, g_oa, gt1, g_ffn, sc2, sh2)


def _up_kernel(h_ref, wg_ref, wu_ref, a_ref, wg_b, wu_b):
    @pl.when(pl.program_id(1) == 0)
    def _():
        wg_b[...] = wg_ref[...].astype(BF16)
        wu_b[...] = wu_ref[...].astype(BF16)

    h = h_ref[...]
    g = _dot(h, wg_b[...])
    u = _dot(h, wu_b[...])
    a_ref[...] = (g * jax.nn.sigmoid(g) * u).astype(BF16)


def _up(h2, w_up, d_ff):
    t, d = h2.shape
    tm, tn = UP_TM, UP_TN
    nj = d_ff // tn
    return pl.pallas_call(
        _up_kernel,
        grid=(nj, t // tm),
        in_specs=[
            pl.BlockSpec((tm, d), lambda j, i: (i, 0)),
            pl.BlockSpec((d, tn), lambda j, i: (0, j)),
            pl.BlockSpec((d, tn), lambda j, i: (0, j + nj)),
        ],
        out_specs=pl.BlockSpec((tm, tn), lambda j, i: (i, j)),
        out_shape=jax.ShapeDtypeStruct((t, d_ff), BF16),
        scratch_shapes=[pltpu.VMEM((d, tn), BF16), pltpu.VMEM((d, tn), BF16)],
        compiler_params=_params(2),
        name="up",
    )(h2, w_up, w_up)


def _down_kernel(a_ref, w_ref, x1_ref, gt_ref, o_ref):
    o_ref[...] = x1_ref[...] + gt_ref[0] * _dot(a_ref[...], w_ref[...])


def _down(a, w_down, x1, gt2, *, seq):
    t, d_ff = a.shape
    d = w_down.shape[1]
    tm, tn = DOWN_TM, DOWN_TN
    spb = seq // tm
    return pl.pallas_call(
        _down_kernel,
        grid=(t // tm, d // tn),
        in_specs=[
            pl.BlockSpec((tm, d_ff), lambda i, j: (i, 0)),
            pl.BlockSpec((d_ff, tn), lambda i, j: (0, j)),
            pl.BlockSpec((tm, tn), lambda i, j: (i, j)),
            pl.BlockSpec((1, 1, tn), lambda i, j: (i // spb, 0, j)),
        ],
        out_specs=pl.BlockSpec((tm, tn), lambda i, j: (i, j)),
        out_shape=jax.ShapeDtypeStruct((t, d), F32),
        compiler_params=_params(2),
        name="down",
    )(a, w_down, x1, gt2)


def kernel(x, c, w_ada, b_ada, g_mix, w_in, conv_w, conv_b, w_gate_a, b_gate_a, w_gate_x, b_gate_x,
           lru_logit, b_forget, g_q, g_k, g_out_rec, g_out_att, w_out, g_ffn, w_up, w_down):
    bsz, seq, d = x.shape
    depth = w_ada.shape[0]
    d_rec = conv_w.shape[-1]
    n_heads = b_forget.shape[-1]
    d_att = n_heads * HEAD_DIM
    n_gate_blocks = w_gate_a.shape[1]
    d_ff = w_down.shape[1]
    t = bsz * seq
    row = lambda v: v.reshape(1, -1)

    for l in range(depth):
        mod = _mod(c, w_ada[l], b_ada[l])
        sh1, sc1, gt1, sh2, sc2, gt2 = [m.reshape(bsz, 1, d) for m in jnp.split(mod, 6, axis=-1)]

        w_in_b = w_in[l].astype(BF16)
        w_f = jnp.pad(w_in_b[:, 2 * d_rec + 3 * d_att:], ((0, 0), (0, LANES - n_heads)))
        wax = jnp.concatenate([w_gate_a[l], w_gate_x[l]], axis=-1).astype(BF16)
        bax = jnp.concatenate([b_gate_a[l].reshape(n_gate_blocks, 1, GATE_BLOCK),
                               b_gate_x[l].reshape(n_gate_blocks, 1, GATE_BLOCK)], axis=-1)
        y_rec, q, k, v, fl = _inproj(x, row(g_mix[l]), sc1, sh1, w_in_b, w_f, row(g_q[l]), row(g_k[l]),
                                     conv_w[l], row(conv_b[l]), wax, bax, row(lru_logit[l]),
                                     d_rec=d_rec, d_att=d_att, n_heads=n_heads)

        f_rows = fl.transpose(0, 2, 1).reshape(bsz * n_heads, seq)
        b_rows = jnp.tile(b_forget[l], bsz).reshape(bsz * n_heads, 1)
        cum = _cum(f_rows, b_rows).reshape(bsz, n_heads, seq // ATTN_T, 1, ATTN_T)
        y_att = _attn(q, k, v, cum)

        x1, h2 = _outproj(y_rec, y_att, x, w_out[l].astype(BF16), row(g_out_rec[l]), row(g_out_att[l]),
                          gt1, row(g_ffn[l]), sc2, sh2)
        a = _up(h2.reshape(t, d), w_up[l], d_ff)
        x = _down(a, w_down[l].astype(BF16), x1.reshape(t, d), gt2, seq=seq).reshape(bsz, seq, d)
    return x
```

```python
import functools
import math

import jax
import jax.numpy as jnp
from jax import lax
from jax.experimental import pallas as pl
from jax.experimental.pallas import tpu as pltpu

F32 = jnp.float32
BF16 = jnp.bfloat16

EPS = 1e-6
LRU_C = 8.0
CONV_W = 4
HEAD_DIM = 128
GATE_BLOCK = 128
LOG2E = math.log2(math.e)

LANES = 128
SUBLANES = 8
VMEM_LIMIT_BYTES = 56 * 1024 * 1024

MOD_TN = 1024
TOKEN_TS = 32
HEADS_PER_DOT = 8
ATTN_T = 256
UP_TM, UP_TN = 1024, 512
DOWN_TM, DOWN_TN = 1024, 512


def _params(n_axes):
    return pltpu.CompilerParams(
        dimension_semantics=("arbitrary",) * n_axes,
        vmem_limit_bytes=VMEM_LIMIT_BYTES,
    )


def _dot(a, b):
    return jnp.dot(a, b, preferred_element_type=F32)


def _rms_scale(x):
    return lax.rsqrt(jnp.mean(x * x, axis=-1, keepdims=True) + EPS)


def _mod_kernel(c_ref, w_ref, b_ref, o_ref):
    c = c_ref[...]
    ca = (c * jax.nn.sigmoid(c)).astype(BF16)
    o_ref[...] = _dot(ca, w_ref[...].astype(BF16)) + b_ref[...]


def _mod(c, w_ada, b_ada):
    bsz, d = c.shape
    n = w_ada.shape[1]
    return pl.pallas_call(
        _mod_kernel,
        grid=(n // MOD_TN,),
        in_specs=[
            pl.BlockSpec((bsz, d), lambda j: (0, 0)),
            pl.BlockSpec((d, MOD_TN), lambda j: (0, j)),
            pl.BlockSpec((1, MOD_TN), lambda j: (0, j)),
        ],
        out_specs=pl.BlockSpec((bsz, MOD_TN), lambda j: (0, j)),
        out_shape=jax.ShapeDtypeStruct((bsz, n), F32),
        compiler_params=_params(1),
        name="mod",
    )(c, w_ada, b_ada.reshape(1, n))


def _sigmoid(x):
    return 1.0 / (1.0 + jnp.exp2(x * -LOG2E))


def _gelu_tanh(y):
    c1 = -2.0 * math.sqrt(2.0 / math.pi) * LOG2E
    c3 = c1 * 0.044715
    return y / (1.0 + jnp.exp2(y * (c1 + c3 * (y * y))))


def _inproj_kernel(x_ref, g_ref, sc_ref, sh_ref, w_ref, wf_ref, gq_ref, gk_ref,
                   cw_ref, cb_ref, wax_ref, bax_ref, lam_ref,
                   yrec_ref, q_ref, k_ref, v_ref, fl_ref,
                   xs_ref, gate_ref, hs_ref, hstate_ref, *, d_rec, d_att, n_heads):
    bsz, ts, d = x_ref.shape
    rows = bsz * ts
    n_slabs = d_rec // LANES
    pad = SUBLANES
    hist = CONV_W - 1
    step = pl.program_id(0)

    @pl.when(step == 0)
    def _():
        xs_ref[:, 0:pad * bsz, :] = jnp.zeros((n_slabs, pad * bsz, LANES), F32)
        hstate_ref[...] = jnp.zeros(hstate_ref.shape, F32)

    @pl.when(step > 0)
    def _():
        xs_ref[:, (pad - hist) * bsz:pad * bsz, :] = xs_ref[:, (pad + ts - hist) * bsz:(pad + ts) * bsz, :]

    x = x_ref[...]
    y = x * _rms_scale(x) * g_ref[...]
    h = (y * (1.0 + sc_ref[...]) + sh_ref[...]).reshape(rows, d).astype(BF16)

    xr = _dot(h, w_ref[:, 0:d_rec])
    for cs in range(n_slabs):
        slab = xs_ref.at[cs]
        for b in range(bsz):
            slab[pl.ds(pad * bsz + b, ts, stride=bsz), :] = xr[b * ts:(b + 1) * ts, cs * LANES:(cs + 1) * LANES]
    gate_ref[...] = _gelu_tanh(_dot(h, w_ref[:, d_rec:2 * d_rec]))
    fl_ref[...] = _dot(h, wf_ref[...])[:, :n_heads].reshape(bsz, ts, n_heads)

    lam = lam_ref[...]
    softplus_neg = jnp.maximum(-lam, 0.0) + jnp.log1p(jnp.exp(-jnp.abs(lam)))
    neg_log_a_gain = LRU_C * softplus_neg
    log2_a_gain = neg_log_a_gain * -LOG2E
    base = (pad - hist) * bsz

    def recurrent_slab(cs):
        ln = slice(cs * GATE_BLOCK, (cs + 1) * GATE_BLOCK)
        u = xs_ref[cs, base:base + rows, :] * cw_ref[0:1, ln]
        for kk in range(1, CONV_W):
            u = u + xs_ref[cs, base + kk * bsz:base + kk * bsz + rows, :] * cw_ref[kk:kk + 1, ln]
        u = u + cb_ref[:, ln]
        g = _dot(u.astype(BF16), wax_ref[cs]) + bax_ref[cs]
        r = _sigmoid(g[:, :GATE_BLOCK])
        i = _sigmoid(g[:, GATE_BLOCK:])
        a = jnp.exp2(r * log2_a_gain[:, ln])
        one_minus_a2 = jnp.tanh(r * neg_log_a_gain[:, ln]) * (1.0 + a * a)
        mult = jnp.where(one_minus_a2 > 0.0, one_minus_a2 * lax.rsqrt(one_minus_a2), 0.0)
        b_in = mult * (i * u)
        hcur = hstate_ref[:, ln]
        hseq = []
        for tt in range(ts):
            rs = slice(tt * bsz, (tt + 1) * bsz)
            hcur = a[rs] * hcur + b_in[rs]
            hseq.append(hcur)
        hstate_ref[:, ln] = hcur
        hs_ref[cs] = jnp.concatenate(hseq, axis=0)
        slab = hs_ref.at[cs]
        for b in range(bsz):
            hb = slab[pl.ds(b, ts, stride=bsz), :]
            yrec_ref[b, :, ln] = (hb * gate_ref[b * ts:(b + 1) * ts, ln]).astype(BF16)

    def project_heads(dst_ref, col0, gain, heads):
        width = len(heads) * HEAD_DIM
        z = _dot(h, w_ref[:, col0 + heads[0] * HEAD_DIM:col0 + heads[0] * HEAD_DIM + width])
        for n, hd in enumerate(heads):
            zh = z[:, n * HEAD_DIM:(n + 1) * HEAD_DIM]
            if gain is not None:
                zh = zh * _rms_scale(zh) * gain
            dst_ref[:, hd] = zh.astype(BF16).reshape(bsz, ts, HEAD_DIM)

    off = 2 * d_rec
    q_gain = gq_ref[...] * (HEAD_DIM ** -0.5 * LOG2E)
    k_gain = gk_ref[...]
    head_groups = [list(range(g0, g0 + HEADS_PER_DOT)) for g0 in range(0, n_heads, HEADS_PER_DOT)]
    dots = ([(q_ref, off, q_gain, hg) for hg in head_groups]
            + [(k_ref, off + d_att, k_gain, hg) for hg in head_groups]
            + [(v_ref, off + 2 * d_att, None, hg) for hg in head_groups])
    slabs = list(range(n_slabs))
    n_dots = len(dots)
    for n, dot_args in enumerate(dots):
        project_heads(*dot_args)
        while slabs and (n_slabs - len(slabs)) * n_dots < (n + 1) * n_slabs:
            recurrent_slab(slabs.pop(0))


def _inproj(x, g_mix, sc1, sh1, w_main, w_f, g_q, g_k, conv_w, conv_b, wax, bax, lam,
            *, d_rec, d_att, n_heads):
    bsz, seq, d = x.shape
    ts = TOKEN_TS
    n_slabs = d_rec // LANES
    const = lambda i: (0, 0)
    const3 = lambda i: (0, 0, 0)
    tchunk = lambda i: (0, i, 0)
    head_shape = jax.ShapeDtypeStruct((bsz, n_heads, seq, HEAD_DIM), BF16)
    head_spec = pl.BlockSpec((bsz, n_heads, ts, HEAD_DIM), lambda i: (0, 0, i, 0))
    return pl.pallas_call(
        functools.partial(_inproj_kernel, d_rec=d_rec, d_att=d_att, n_heads=n_heads),
        grid=(seq // ts,),
        in_specs=[
            pl.BlockSpec((bsz, ts, d), tchunk),
            pl.BlockSpec((1, d), const),
            pl.BlockSpec((bsz, 1, d), const3),
            pl.BlockSpec((bsz, 1, d), const3),
            pl.BlockSpec((d, 2 * d_rec + 3 * d_att), const, pipeline_mode=pl.Buffered(1)),
            pl.BlockSpec(w_f.shape, const, pipeline_mode=pl.Buffered(1)),
            pl.BlockSpec((1, HEAD_DIM), const),
            pl.BlockSpec((1, HEAD_DIM), const),
            pl.BlockSpec(conv_w.shape, const),
            pl.BlockSpec(conv_b.shape, const),
            pl.BlockSpec(wax.shape, const3),
            pl.BlockSpec(bax.shape, const3),
            pl.BlockSpec(lam.shape, const),
        ],
        out_specs=[
            pl.BlockSpec((bsz, ts, d_rec), tchunk),
            head_spec, head_spec, head_spec,
            pl.BlockSpec((bsz, ts, n_heads), tchunk),
        ],
        out_shape=[
            jax.ShapeDtypeStruct((bsz, seq, d_rec), BF16),
            head_shape, head_shape, head_shape,
            jax.ShapeDtypeStruct((bsz, seq, n_heads), F32),
        ],
        scratch_shapes=[
            pltpu.VMEM((n_slabs, (SUBLANES + ts) * bsz, LANES), F32),
            pltpu.VMEM((bsz * ts, d_rec), F32),
            pltpu.VMEM((n_slabs, ts * bsz, LANES), F32),
            pltpu.VMEM((bsz, d_rec), F32),
        ],
        compiler_params=_params(1),
        name="inproj",
    )(x, g_mix, sc1, sh1, w_main, w_f, g_q, g_k, conv_w, conv_b, wax, bax, lam)


def _cum_kernel(f_ref, b_ref, o_ref):
    z = f_ref[...] + b_ref[...]
    x = -(jnp.maximum(-z, 0.0) + jnp.log1p(jnp.exp(-jnp.abs(z))))
    n = x.shape[-1]
    lane = lax.broadcasted_iota(jnp.int32, x.shape, 1)
    d = 1
    while d < n:
        x = x + jnp.where(lane >= d, pltpu.roll(x, d, axis=1), 0.0)
        d *= 2
    o_ref[...] = x


def _cum(f_rows, b_rows):
    r, n = f_rows.shape
    return pl.pallas_call(
        _cum_kernel,
        grid=(1,),
        in_specs=[pl.BlockSpec((r, n), lambda i: (0, 0)),
                  pl.BlockSpec((r, 1), lambda i: (0, 0))],
        out_specs=pl.BlockSpec((r, n), lambda i: (0, 0)),
        out_shape=jax.ShapeDtypeStruct((r, n), F32),
        compiler_params=_params(1),
        name="cum",
    )(f_rows, b_rows)


def _attn_kernel(q_ref, k_ref, v_ref, cum_ref, o_ref):
    seq = q_ref.shape[2]
    t = ATTN_T
    n_blocks = seq // t
    row = lax.broadcasted_iota(jnp.int32, (t, t), 0)
    col = lax.broadcasted_iota(jnp.int32, (t, t), 1)
    causal = col <= row
    ones_cols = jnp.ones((t, HEAD_DIM), BF16)

    m = acc = None
    for kj in range(n_blocks):
        blk = slice(kj * t, (kj + 1) * t)
        last = kj == n_blocks - 1
        q = q_ref[0, 0, kj * t:, :]
        kb = k_ref[0, 0, blk, :]
        vb = jnp.concatenate([v_ref[0, 0, blk, :], ones_cols], axis=1)
        s = lax.dot_general(q, kb, (((1,), (1,)), ((), ())),
                            preferred_element_type=F32) - cum_ref[0, 0, kj] * LOG2E
        s_diag = jnp.where(causal, s[:t], -jnp.inf)
        s = s_diag if last else jnp.concatenate([s_diag, s[t:]], axis=0)
        m_blk = jnp.max(s, axis=1, keepdims=True)
        if kj == 0:
            m_new = m_blk
            acc_new = _dot(jnp.exp2(s - m_new).astype(BF16), vb)
        else:
            m_new = jnp.maximum(m, m_blk)
            acc_new = jnp.exp2(m - m_new) * acc + _dot(jnp.exp2(s - m_new).astype(BF16), vb)
        o_ref[0, 0, blk, :] = (acc_new[:t, :HEAD_DIM] / acc_new[:t, HEAD_DIM:]).astype(BF16)
        if not last:
            m, acc = m_new[t:], acc_new[t:]


def _attn(q, k, v, cum):
    bsz, n_heads, seq, dh = q.shape
    cum5 = cum.reshape(bsz, n_heads, seq // ATTN_T, 1, ATTN_T)
    spec = pl.BlockSpec((1, 1, seq, dh), lambda b, h: (b, h, 0, 0))
    return pl.pallas_call(
        _attn_kernel,
        grid=(bsz, n_heads),
        in_specs=[spec, spec, spec,
                  pl.BlockSpec((1, 1) + cum5.shape[2:], lambda b, h: (b, h, 0, 0, 0))],
        out_specs=spec,
        out_shape=jax.ShapeDtypeStruct(q.shape, BF16),
        compiler_params=_params(2),
        name="attn",
    )(q, k, v, cum5)


def _outproj_kernel(yrec_ref, yatt_ref, x_ref, w_ref, gor_ref, goa_ref, gt_ref,
                    gf_ref, sc_ref, sh_ref, x1_ref, h2_ref, *, n_heads):
    bsz, ts, d = x_ref.shape
    yr = yrec_ref[...].astype(F32).reshape(bsz * ts, yrec_ref.shape[2])
    parts = [(yr * _rms_scale(yr) * gor_ref[...]).astype(BF16)]
    heads = [yatt_ref[:, hd].astype(F32).reshape(bsz * ts, HEAD_DIM) for hd in range(n_heads)]
    ssq = heads[0] * heads[0]
    for hh in heads[1:]:
        ssq = ssq + hh * hh
    inv = lax.rsqrt(jnp.sum(ssq, axis=-1, keepdims=True) / (n_heads * HEAD_DIM) + EPS)
    for hd in range(n_heads):
        gain = goa_ref[:, hd * HEAD_DIM:(hd + 1) * HEAD_DIM]
        parts.append((heads[hd] * inv * gain).astype(BF16))
    mix = jnp.concatenate(parts, axis=1)
    x1 = x_ref[...] + gt_ref[...] * _dot(mix, w_ref[...]).reshape(bsz, ts, d)
    x1_ref[...] = x1
    y = x1 * _rms_scale(x1) * gf_ref[...]
    h2_ref[...] = (y * (1.0 + sc_ref[...]) + sh_ref[...]).astype(BF16)


def _outproj(y_rec, y_att, x, w_out, g_or, g_oa, gt1, g_ffn, sc2, sh2):
    bsz, seq, d = x.shape
    _, n_heads, _, dh = y_att.shape
    d_rec = y_rec.shape[2]
    ts = TOKEN_TS
    const = lambda i: (0, 0)
    const3 = lambda i: (0, 0, 0)
    tchunk = lambda i: (0, i, 0)
    tok_spec = pl.BlockSpec((bsz, ts, d), tchunk)
    mod_spec = pl.BlockSpec((bsz, 1, d), const3)
    return pl.pallas_call(
        functools.partial(_outproj_kernel, n_heads=n_heads),
        grid=(seq // ts,),
        in_specs=[
            pl.BlockSpec((bsz, ts, d_rec), tchunk),
            pl.BlockSpec((bsz, n_heads, ts, dh), lambda i: (0, 0, i, 0)),
            tok_spec,
            pl.BlockSpec(w_out.shape, const, pipeline_mode=pl.Buffered(1)),
            pl.BlockSpec((1, d_rec), const),
            pl.BlockSpec((1, n_heads * dh), const),
            mod_spec,
            pl.BlockSpec((1, d), const),
            mod_spec,
            mod_spec,
        ],
        out_specs=[tok_spec, tok_spec],
        out_shape=[jax.ShapeDtypeStruct(x.shape, F32), jax.ShapeDtypeStruct(x.shape, BF16)],
        compiler_params=_params(1),
        name="outproj",
    )(y_rec, y_att, x, w_out, g_or, g_oa, gt1, g_ffn, sc2, sh2)


def _up_kernel(h_ref, wg_ref, wu_ref, a_ref, wg_b, wu_b):
    @pl.when(pl.program_id(1) == 0)
    def _():
        wg_b[...] = wg_ref[...].astype(BF16)
        wu_b[...] = wu_ref[...].astype(BF16)

    h = h_ref[...]
    g = _dot(h, wg_b[...])
    u = _dot(h, wu_b[...])
    a_ref[...] = (g * jax.nn.sigmoid(g) * u).astype(BF16)


def _up(h2, w_up, d_ff):
    t, d = h2.shape
    tm, tn = UP_TM, UP_TN
    nj = d_ff // tn
    return pl.pallas_call(
        _up_kernel,
        grid=(nj, t // tm),
        in_specs=[
            pl.BlockSpec((tm, d), lambda j, i: (i, 0)),
            pl.BlockSpec((d, tn), lambda j, i: (0, j)),
            pl.BlockSpec((d, tn), lambda j, i: (0, j + nj)),
        ],
        out_specs=pl.BlockSpec((tm, tn), lambda j, i: (i, j)),
        out_shape=jax.ShapeDtypeStruct((t, d_ff), BF16),
        scratch_shapes=[pltpu.VMEM((d, tn), BF16), pltpu.VMEM((d, tn), BF16)],
        compiler_params=_params(2),
        name="up",
    )(h2, w_up, w_up)


def _down_kernel(a_ref, w_ref, x1_ref, gt_ref, o_ref):
    o_ref[...] = x1_ref[...] + gt_ref[0] * _dot(a_ref[...], w_ref[...])


def _down(a, w_down, x1, gt2, *, seq):
    t, d_ff = a.shape
    d = w_down.shape[1]
    tm, tn = DOWN_TM, DOWN_TN
    spb = seq // tm
    return pl.pallas_call(
        _down_kernel,
        grid=(t // tm, d // tn),
        in_specs=[
            pl.BlockSpec((tm, d_ff), lambda i, j: (i, 0)),
            pl.BlockSpec((d_ff, tn), lambda i, j: (0, j)),
            pl.BlockSpec((tm, tn), lambda i, j: (i, j)),
            pl.BlockSpec((1, 1, tn), lambda i, j: (i // spb, 0, j)),
        ],
        out_specs=pl.BlockSpec((tm, tn), lambda i, j: (i, j)),
        out_shape=jax.ShapeDtypeStruct((t, d), F32),
        compiler_params=_params(2),
        name="down",
    )(a, w_down, x1, gt2)


def kernel(x, c, w_ada, b_ada, g_mix, w_in, conv_w, conv_b, w_gate_a, b_gate_a, w_gate_x, b_gate_x,
           lru_logit, b_forget, g_q, g_k, g_out_rec, g_out_att, w_out, g_ffn, w_up, w_down):
    bsz, seq, d = x.shape
    depth = w_ada.shape[0]
    d_rec = conv_w.shape[-1]
    n_heads = b_forget.shape[-1]
    d_att = n_heads * HEAD_DIM
    n_gate_blocks = w_gate_a.shape[1]
    d_ff = w_down.shape[1]
    t = bsz * seq
    row = lambda v: v.reshape(1, -1)

    for l in range(depth):
        mod = _mod(c, w_ada[l], b_ada[l])
        sh1, sc1, gt1, sh2, sc2, gt2 = [m.reshape(bsz, 1, d) for m in jnp.split(mod, 6, axis=-1)]

        w_in_b = w_in[l].astype(BF16)
        w_f = jnp.pad(w_in_b[:, 2 * d_rec + 3 * d_att:], ((0, 0), (0, LANES - n_heads)))
        wax = jnp.concatenate([w_gate_a[l], w_gate_x[l]], axis=-1).astype(BF16)
        bax = jnp.concatenate([b_gate_a[l].reshape(n_gate_blocks, 1, GATE_BLOCK),
                               b_gate_x[l].reshape(n_gate_blocks, 1, GATE_BLOCK)], axis=-1)
        y_rec, q, k, v, fl = _inproj(x, row(g_mix[l]), sc1, sh1, w_in_b, w_f, row(g_q[l]), row(g_k[l]),
                                     conv_w[l], row(conv_b[l]), wax, bax, row(lru_logit[l]),
                                     d_rec=d_rec, d_att=d_att, n_heads=n_heads)

        f_rows = fl.transpose(0, 2, 1).reshape(bsz * n_heads, seq)
        b_rows = jnp.tile(b_forget[l], bsz).reshape(bsz * n_heads, 1)
        cum = _cum(f_rows, b_rows)
        y_att = _attn(q, k, v, cum)

        x1, h2 = _outproj(y_rec, y_att, x, w_out[l].astype(BF16), row(g_out_rec[l]), row(g_out_att[l]),
                          gt1, row(g_ffn[l]), sc2, sh2)
        a = _up(h2.reshape(t, d), w_up[l], d_ff)
        x = _down(a, w_down[l].astype(BF16), x1.reshape(t, d), gt2, seq=seq).reshape(bsz, seq, d)
    return x
```

```python
import functools
import math

import jax
import jax.numpy as jnp
from jax import lax
from jax.experimental import pallas as pl
from jax.experimental.pallas import tpu as pltpu

F32 = jnp.float32
BF16 = jnp.bfloat16

EPS = 1e-6
LRU_C = 8.0
CONV_W = 4
HEAD_DIM = 128
GATE_BLOCK = 128
LOG2E = math.log2(math.e)

LANES = 128
SUBLANES = 8
VMEM_LIMIT_BYTES = 56 * 1024 * 1024

MOD_TN = 1024
TOKEN_TS = 32
HEADS_PER_DOT = 8
ATTN_T = 256
ATTN_HEADS_PER_STEP = 4
UP_TM, UP_TN = 1024, 512
DOWN_TM, DOWN_TN = 1024, 512


def _params(n_axes):
    return pltpu.CompilerParams(
        dimension_semantics=("arbitrary",) * n_axes,
        vmem_limit_bytes=VMEM_LIMIT_BYTES,
    )


def _dot(a, b):
    return jnp.dot(a, b, preferred_element_type=F32)


def _rms_scale(x):
    return lax.rsqrt(jnp.mean(x * x, axis=-1, keepdims=True) + EPS)


def _mod_kernel(c_ref, w_ref, b_ref, o_ref):
    c = c_ref[...]
    ca = (c * jax.nn.sigmoid(c)).astype(BF16)
    o_ref[...] = _dot(ca, w_ref[...].astype(BF16)) + b_ref[...]


def _mod(c, w_ada, b_ada):
    bsz, d = c.shape
    n = w_ada.shape[1]
    return pl.pallas_call(
        _mod_kernel,
        grid=(n // MOD_TN,),
        in_specs=[
            pl.BlockSpec((bsz, d), lambda j: (0, 0)),
            pl.BlockSpec((d, MOD_TN), lambda j: (0, j)),
            pl.BlockSpec((1, MOD_TN), lambda j: (0, j)),
        ],
        out_specs=pl.BlockSpec((bsz, MOD_TN), lambda j: (0, j)),
        out_shape=jax.ShapeDtypeStruct((bsz, n), F32),
        compiler_params=_params(1),
        name="mod",
    )(c, w_ada, b_ada.reshape(1, n))


def _sigmoid(x):
    return 1.0 / (1.0 + jnp.exp2(x * -LOG2E))


def _gelu_tanh(y):
    c1 = -2.0 * math.sqrt(2.0 / math.pi) * LOG2E
    c3 = c1 * 0.044715
    return y / (1.0 + jnp.exp2(y * (c1 + c3 * (y * y))))


def _inproj_kernel(x_ref, g_ref, sc_ref, sh_ref, w_ref, wf_ref, gq_ref, gk_ref,
                   cw_ref, cb_ref, wax_ref, bax_ref, lam_ref,
                   yrec_ref, q_ref, k_ref, v_ref, fl_ref,
                   xs_ref, gate_ref, hs_ref, hstate_ref, *, d_rec, d_att, n_heads):
    bsz, ts, d = x_ref.shape
    rows = bsz * ts
    n_slabs = d_rec // LANES
    pad = SUBLANES
    hist = CONV_W - 1
    step = pl.program_id(0)

    @pl.when(step == 0)
    def _():
        xs_ref[:, 0:pad * bsz, :] = jnp.zeros((n_slabs, pad * bsz, LANES), F32)
        hstate_ref[...] = jnp.zeros(hstate_ref.shape, F32)

    @pl.when(step > 0)
    def _():
        xs_ref[:, (pad - hist) * bsz:pad * bsz, :] = xs_ref[:, (pad + ts - hist) * bsz:(pad + ts) * bsz, :]

    x = x_ref[...]
    y = x * _rms_scale(x) * g_ref[...]
    h = (y * (1.0 + sc_ref[...]) + sh_ref[...]).reshape(rows, d).astype(BF16)

    xr = _dot(h, w_ref[:, 0:d_rec])
    for cs in range(n_slabs):
        slab = xs_ref.at[cs]
        for b in range(bsz):
            slab[pl.ds(pad * bsz + b, ts, stride=bsz), :] = xr[b * ts:(b + 1) * ts, cs * LANES:(cs + 1) * LANES]
    gate_ref[...] = _gelu_tanh(_dot(h, w_ref[:, d_rec:2 * d_rec]))
    fl_ref[...] = _dot(h, wf_ref[...])[:, :n_heads].reshape(bsz, ts, n_heads)

    lam = lam_ref[...]
    softplus_neg = jnp.maximum(-lam, 0.0) + jnp.log1p(jnp.exp(-jnp.abs(lam)))
    neg_log_a_gain = LRU_C * softplus_neg
    log2_a_gain = neg_log_a_gain * -LOG2E
    base = (pad - hist) * bsz

    def recurrent_slab(cs):
        ln = slice(cs * GATE_BLOCK, (cs + 1) * GATE_BLOCK)
        u = xs_ref[cs, base:base + rows, :] * cw_ref[0:1, ln]
        for kk in range(1, CONV_W):
            u = u + xs_ref[cs, base + kk * bsz:base + kk * bsz + rows, :] * cw_ref[kk:kk + 1, ln]
        u = u + cb_ref[:, ln]
        g = _dot(u.astype(BF16), wax_ref[cs]) + bax_ref[cs]
        r = _sigmoid(g[:, :GATE_BLOCK])
        i = _sigmoid(g[:, GATE_BLOCK:])
        a = jnp.exp2(r * log2_a_gain[:, ln])
        one_minus_a2 = jnp.tanh(r * neg_log_a_gain[:, ln]) * (1.0 + a * a)
        mult = jnp.where(one_minus_a2 > 0.0, one_minus_a2 * lax.rsqrt(one_minus_a2), 0.0)
        b_in = mult * (i * u)
        hcur = hstate_ref[:, ln]
        hseq = []
        for tt in range(ts):
            rs = slice(tt * bsz, (tt + 1) * bsz)
            hcur = a[rs] * hcur + b_in[rs]
            hseq.append(hcur)
        hstate_ref[:, ln] = hcur
        hs_ref[cs] = jnp.concatenate(hseq, axis=0)
        slab = hs_ref.at[cs]
        for b in range(bsz):
            hb = slab[pl.ds(b, ts, stride=bsz), :]
            yrec_ref[b, :, ln] = (hb * gate_ref[b * ts:(b + 1) * ts, ln]).astype(BF16)

    def project_heads(dst_ref, col0, gain, heads):
        width = len(heads) * HEAD_DIM
        z = _dot(h, w_ref[:, col0 + heads[0] * HEAD_DIM:col0 + heads[0] * HEAD_DIM + width])
        for n, hd in enumerate(heads):
            zh = z[:, n * HEAD_DIM:(n + 1) * HEAD_DIM]
            if gain is not None:
                zh = zh * _rms_scale(zh) * gain
            dst_ref[:, hd] = zh.astype(BF16).reshape(bsz, ts, HEAD_DIM)

    off = 2 * d_rec
    q_gain = gq_ref[...] * (HEAD_DIM ** -0.5 * LOG2E)
    k_gain = gk_ref[...]
    head_groups = [list(range(g0, g0 + HEADS_PER_DOT)) for g0 in range(0, n_heads, HEADS_PER_DOT)]
    dots = ([(q_ref, off, q_gain, hg) for hg in head_groups]
            + [(k_ref, off + d_att, k_gain, hg) for hg in head_groups]
            + [(v_ref, off + 2 * d_att, None, hg) for hg in head_groups])
    slabs = list(range(n_slabs))
    n_dots = len(dots)
    for n, dot_args in enumerate(dots):
        project_heads(*dot_args)
        while slabs and (n_slabs - len(slabs)) * n_dots < (n + 1) * n_slabs:
            recurrent_slab(slabs.pop(0))


def _inproj(x, g_mix, sc1, sh1, w_main, w_f, g_q, g_k, conv_w, conv_b, wax, bax, lam,
            *, d_rec, d_att, n_heads):
    bsz, seq, d = x.shape
    ts = TOKEN_TS
    n_slabs = d_rec // LANES
    const = lambda i: (0, 0)
    const3 = lambda i: (0, 0, 0)
    tchunk = lambda i: (0, i, 0)
    head_shape = jax.ShapeDtypeStruct((bsz, n_heads, seq, HEAD_DIM), BF16)
    head_spec = pl.BlockSpec((bsz, n_heads, ts, HEAD_DIM), lambda i: (0, 0, i, 0))
    return pl.pallas_call(
        functools.partial(_inproj_kernel, d_rec=d_rec, d_att=d_att, n_heads=n_heads),
        grid=(seq // ts,),
        in_specs=[
            pl.BlockSpec((bsz, ts, d), tchunk),
            pl.BlockSpec((1, d), const),
            pl.BlockSpec((bsz, 1, d), const3),
            pl.BlockSpec((bsz, 1, d), const3),
            pl.BlockSpec((d, 2 * d_rec + 3 * d_att), const, pipeline_mode=pl.Buffered(1)),
            pl.BlockSpec(w_f.shape, const, pipeline_mode=pl.Buffered(1)),
            pl.BlockSpec((1, HEAD_DIM), const),
            pl.BlockSpec((1, HEAD_DIM), const),
            pl.BlockSpec(conv_w.shape, const),
            pl.BlockSpec(conv_b.shape, const),
            pl.BlockSpec(wax.shape, const3),
            pl.BlockSpec(bax.shape, const3),
            pl.BlockSpec(lam.shape, const),
        ],
        out_specs=[
            pl.BlockSpec((bsz, ts, d_rec), tchunk),
            head_spec, head_spec, head_spec,
            pl.BlockSpec((bsz, ts, n_heads), tchunk),
        ],
        out_shape=[
            jax.ShapeDtypeStruct((bsz, seq, d_rec), BF16),
            head_shape, head_shape, head_shape,
            jax.ShapeDtypeStruct((bsz, seq, n_heads), F32),
        ],
        scratch_shapes=[
            pltpu.VMEM((n_slabs, (SUBLANES + ts) * bsz, LANES), F32),
            pltpu.VMEM((bsz * ts, d_rec), F32),
            pltpu.VMEM((n_slabs, ts * bsz, LANES), F32),
            pltpu.VMEM((bsz, d_rec), F32),
        ],
        compiler_params=_params(1),
        name="inproj",
    )(x, g_mix, sc1, sh1, w_main, w_f, g_q, g_k, conv_w, conv_b, wax, bax, lam)


def _cum_kernel(f_ref, b_ref, o_ref):
    z = f_ref[...] + b_ref[...]
    x = -(jnp.maximum(-z, 0.0) + jnp.log1p(jnp.exp(-jnp.abs(z))))
    n = x.shape[-1]
    lane = lax.broadcasted_iota(jnp.int32, x.shape, 1)
    d = 1
    while d < n:
        x = x + jnp.where(lane >= d, pltpu.roll(x, d, axis=1), 0.0)
        d *= 2
    o_ref[...] = x


def _cum(f_rows, b_rows):
    r, n = f_rows.shape
    return pl.pallas_call(
        _cum_kernel,
        grid=(1,),
        in_specs=[pl.BlockSpec((r, n), lambda i: (0, 0)),
                  pl.BlockSpec((r, 1), lambda i: (0, 0))],
        out_specs=pl.BlockSpec((r, n), lambda i: (0, 0)),
        out_shape=jax.ShapeDtypeStruct((r, n), F32),
        compiler_params=_params(1),
        name="cum",
    )(f_rows, b_rows)


def _attn_kernel(q_ref, k_ref, v_ref, cum_ref, o_ref):
    seq = q_ref.shape[2]
    t = ATTN_T
    n_blocks = seq // t
    row = lax.broadcasted_iota(jnp.int32, (t, t), 0)
    col = lax.broadcasted_iota(jnp.int32, (t, t), 1)
    causal = col <= row
    ones_cols = jnp.ones((t, HEAD_DIM), BF16)

    n_h = q_ref.shape[1]
    m = [None] * n_h
    acc = [None] * n_h
    for kj in range(n_blocks):
        blk = slice(kj * t, (kj + 1) * t)
        last = kj == n_blocks - 1
        for hh in range(n_h):
            q = q_ref[0, hh, kj * t:, :]
            kb = k_ref[0, hh, blk, :]
            vb = jnp.concatenate([v_ref[0, hh, blk, :], ones_cols], axis=1)
            s = lax.dot_general(q, kb, (((1,), (1,)), ((), ())),
                                preferred_element_type=F32) - cum_ref[0, hh, kj] * LOG2E
            s_diag = jnp.where(causal, s[:t], -jnp.inf)
            s = s_diag if last else jnp.concatenate([s_diag, s[t:]], axis=0)
            m_blk = jnp.max(s, axis=1, keepdims=True)
            if kj == 0:
                m_new = m_blk
                acc_new = _dot(jnp.exp2(s - m_new).astype(BF16), vb)
            else:
                m_new = jnp.maximum(m[hh], m_blk)
                acc_new = jnp.exp2(m[hh] - m_new) * acc[hh] + _dot(jnp.exp2(s - m_new).astype(BF16), vb)
            o_ref[0, hh, blk, :] = (acc_new[:t, :HEAD_DIM] / acc_new[:t, HEAD_DIM:]).astype(BF16)
            if not last:
                m[hh], acc[hh] = m_new[t:], acc_new[t:]


def _attn(q, k, v, cum):
    bsz, n_heads, seq, dh = q.shape
    cum5 = cum.reshape(bsz, n_heads, seq // ATTN_T, 1, ATTN_T)
    hps = ATTN_HEADS_PER_STEP
    spec = pl.BlockSpec((1, hps, seq, dh), lambda b, h: (b, h, 0, 0))
    return pl.pallas_call(
        _attn_kernel,
        grid=(bsz, n_heads // hps),
        in_specs=[spec, spec, spec,
                  pl.BlockSpec((1, hps) + cum5.shape[2:], lambda b, h: (b, h, 0, 0, 0))],
        out_specs=spec,
        out_shape=jax.ShapeDtypeStruct(q.shape, BF16),
        compiler_params=_params(2),
        name="attn",
    )(q, k, v, cum5)


def _outproj_kernel(yrec_ref, yatt_ref, x_ref, w_ref, gor_ref, goa_ref, gt_ref,
                    gf_ref, sc_ref, sh_ref, x1_ref, h2_ref, *, n_heads):
    bsz, ts, d = x_ref.shape
    yr = yrec_ref[...].astype(F32).reshape(bsz * ts, yrec_ref.shape[2])
    parts = [(yr * _rms_scale(yr) * gor_ref[...]).astype(BF16)]
    heads = [yatt_ref[:, hd].astype(F32).reshape(bsz * ts, HEAD_DIM) for hd in range(n_heads)]
    ssq = heads[0] * heads[0]
    for hh in heads[1:]:
        ssq = ssq + hh * hh
    inv = lax.rsqrt(jnp.sum(ssq, axis=-1, keepdims=True) / (n_heads * HEAD_DIM) + EPS)
    for hd in range(n_heads):
        gain = goa_ref[:, hd * HEAD_DIM:(hd + 1) * HEAD_DIM]
        parts.append((heads[hd] * inv * gain).astype(BF16))
    mix = jnp.concatenate(parts, axis=1)
    x1 = x_ref[...] + gt_ref[...] * _dot(mix, w_ref[...]).reshape(bsz, ts, d)
    x1_ref[...] = x1
    y = x1 * _rms_scale(x1) * gf_ref[...]
    h2_ref[...] = (y * (1.0 + sc_ref[...]) + sh_ref[...]).astype(BF16)


def _outproj(y_rec, y_att, x, w_out, g_or, g_oa, gt1, g_ffn, sc2, sh2):
    bsz, seq, d = x.shape
    _, n_heads, _, dh = y_att.shape
    d_rec = y_rec.shape[2]
    ts = TOKEN_TS
    const = lambda i: (0, 0)
    const3 = lambda i: (0, 0, 0)
    tchunk = lambda i: (0, i, 0)
    tok_spec = pl.BlockSpec((bsz, ts, d), tchunk)
    mod_spec = pl.BlockSpec((bsz, 1, d), const3)
    return pl.pallas_call(
        functools.partial(_outproj_kernel, n_heads=n_heads),
        grid=(seq // ts,),
        in_specs=[
            pl.BlockSpec((bsz, ts, d_rec), tchunk),
            pl.BlockSpec((bsz, n_heads, ts, dh), lambda i: (0, 0, i, 0)),
            tok_spec,
            pl.BlockSpec(w_out.shape, const, pipeline_mode=pl.Buffered(1)),
            pl.BlockSpec((1, d_rec), const),
            pl.BlockSpec((1, n_heads * dh), const),
            mod_spec,
            pl.BlockSpec((1, d), const),
            mod_spec,
            mod_spec,
        ],
        out_specs=[tok_spec, tok_spec],
        out_shape=[jax.ShapeDtypeStruct(x.shape, F32), jax.ShapeDtypeStruct(x.shape, BF16)],
        compiler_params=_params(1),
        name="outproj",
    )(y_rec, y_att, x, w_out, g_or, g_oa, gt1, g_ffn, sc2, sh2)


def _up_kernel(h_ref, wg_ref, wu_ref, a_ref, wg_b, wu_b):
    @pl.when(pl.program_id(1) == 0)
    def _():
        wg_b[...] = wg_ref[...].astype(BF16)
        wu_b[...] = wu_ref[...].astype(BF16)

    h = h_ref[...]
    g = _dot(h, wg_b[...])
    u = _dot(h, wu_b[...])
    a_ref[...] = (g * jax.nn.sigmoid(g) * u).astype(BF16)


def _up(h2, w_up, d_ff):
    t, d = h2.shape
    tm, tn = UP_TM, UP_TN
    nj = d_ff // tn
    return pl.pallas_call(
        _up_kernel,
        grid=(nj, t // tm),
        in_specs=[
            pl.BlockSpec((tm, d), lambda j, i: (i, 0)),
            pl.BlockSpec((d, tn), lambda j, i: (0, j)),
            pl.BlockSpec((d, tn), lambda j, i: (0, j + nj)),
        ],
        out_specs=pl.BlockSpec((tm, tn), lambda j, i: (i, j)),
        out_shape=jax.ShapeDtypeStruct((t, d_ff), BF16),
        scratch_shapes=[pltpu.VMEM((d, tn), BF16), pltpu.VMEM((d, tn), BF16)],
        compiler_params=_params(2),
        name="up",
    )(h2, w_up, w_up)


def _down_kernel(a_ref, w_ref, x1_ref, gt_ref, o_ref):
    o_ref[...] = x1_ref[...] + gt_ref[0] * _dot(a_ref[...], w_ref[...])


def _down(a, w_down, x1, gt2, *, seq):
    t, d_ff = a.shape
    d = w_down.shape[1]
    tm, tn = DOWN_TM, DOWN_TN
    spb = seq // tm
    return pl.pallas_call(
        _down_kernel,
        grid=(t // tm, d // tn),
        in_specs=[
            pl.BlockSpec((tm, d_ff), lambda i, j: (i, 0)),
            pl.BlockSpec((d_ff, tn), lambda i, j: (0, j)),
            pl.BlockSpec((tm, tn), lambda i, j: (i, j)),
            pl.BlockSpec((1, 1, tn), lambda i, j: (i // spb, 0, j)),
        ],
        out_specs=pl.BlockSpec((tm, tn), lambda i, j: (i, j)),
        out_shape=jax.ShapeDtypeStruct((t, d), F32),
        compiler_params=_params(2),
        name="down",
    )(a, w_down, x1, gt2)


def kernel(x, c, w_ada, b_ada, g_mix, w_in, conv_w, conv_b, w_gate_a, b_gate_a, w_gate_x, b_gate_x,
           lru_logit, b_forget, g_q, g_k, g_out_rec, g_out_att, w_out, g_ffn, w_up, w_down):
    bsz, seq, d = x.shape
    depth = w_ada.shape[0]
    d_rec = conv_w.shape[-1]
    n_heads = b_forget.shape[-1]
    d_att = n_heads * HEAD_DIM
    n_gate_blocks = w_gate_a.shape[1]
    d_ff = w_down.shape[1]
    t = bsz * seq
    row = lambda v: v.reshape(1, -1)

    for l in range(depth):
        mod = _mod(c, w_ada[l], b_ada[l])
        sh1, sc1, gt1, sh2, sc2, gt2 = [m.reshape(bsz, 1, d) for m in jnp.split(mod, 6, axis=-1)]

        w_in_b = w_in[l].astype(BF16)
        w_f = jnp.pad(w_in_b[:, 2 * d_rec + 3 * d_att:], ((0, 0), (0, LANES - n_heads)))
        wax = jnp.concatenate([w_gate_a[l], w_gate_x[l]], axis=-1).astype(BF16)
        bax = jnp.concatenate([b_gate_a[l].reshape(n_gate_blocks, 1, GATE_BLOCK),
                               b_gate_x[l].reshape(n_gate_blocks, 1, GATE_BLOCK)], axis=-1)
        y_rec, q, k, v, fl = _inproj(x, row(g_mix[l]), sc1, sh1, w_in_b, w_f, row(g_q[l]), row(g_k[l]),
                                     conv_w[l], row(conv_b[l]), wax, bax, row(lru_logit[l]),
                                     d_rec=d_rec, d_att=d_att, n_heads=n_heads)

        f_rows = fl.transpose(0, 2, 1).reshape(bsz * n_heads, seq)
        b_rows = jnp.tile(b_forget[l], bsz).reshape(bsz * n_heads, 1)
        cum = _cum(f_rows, b_rows)
        y_att = _attn(q, k, v, cum)

        x1, h2 = _outproj(y_rec, y_att, x, w_out[l].astype(BF16), row(g_out_rec[l]), row(g_out_att[l]),
                          gt1, row(g_ffn[l]), sc2, sh2)
        a = _up(h2.reshape(t, d), w_up[l], d_ff)
        x = _down(a, w_down[l].astype(BF16), x1.reshape(t, d), gt2, seq=seq).reshape(bsz, seq, d)
    return x
```

```python
import functools
import math

import jax
import jax.numpy as jnp
from jax import lax
from jax.experimental import pallas as pl
from jax.experimental.pallas import tpu as pltpu

F32 = jnp.float32
BF16 = jnp.bfloat16

EPS = 1e-6
LRU_C = 8.0
CONV_W = 4
HEAD_DIM = 128
GATE_BLOCK = 128
LOG2E = math.log2(math.e)

LANES = 128
SUBLANES = 8
VMEM_LIMIT_BYTES = 56 * 1024 * 1024

MOD_TN = 1024
TOKEN_TS = 32
HEADS_PER_DOT = 8
ATTN_T = 256
ATTN_HEADS_PER_STEP = 4
UP_TM, UP_TN = 1024, 512
DOWN_TM, DOWN_TN = 1024, 512


def _params(n_axes):
    return pltpu.CompilerParams(
        dimension_semantics=("arbitrary",) * n_axes,
        vmem_limit_bytes=VMEM_LIMIT_BYTES,
    )


def _dot(a, b):
    return jnp.dot(a, b, preferred_element_type=F32)


def _rms_scale(x):
    return lax.rsqrt(jnp.mean(x * x, axis=-1, keepdims=True) + EPS)


def _mod_kernel(c_ref, w_ref, b_ref, o_ref):
    c = c_ref[...]
    ca = (c * jax.nn.sigmoid(c)).astype(BF16)
    o_ref[...] = _dot(ca, w_ref[...].astype(BF16)) + b_ref[...]


def _mod(c, w_ada, b_ada):
    bsz, d = c.shape
    n = w_ada.shape[1]
    return pl.pallas_call(
        _mod_kernel,
        grid=(n // MOD_TN,),
        in_specs=[
            pl.BlockSpec((bsz, d), lambda j: (0, 0)),
            pl.BlockSpec((d, MOD_TN), lambda j: (0, j)),
            pl.BlockSpec((1, MOD_TN), lambda j: (0, j)),
        ],
        out_specs=pl.BlockSpec((bsz, MOD_TN), lambda j: (0, j)),
        out_shape=jax.ShapeDtypeStruct((bsz, n), F32),
        compiler_params=_params(1),
        name="mod",
    )(c, w_ada, b_ada.reshape(1, n))


def _sigmoid(x):
    return 1.0 / (1.0 + jnp.exp2(x * -LOG2E))


def _gelu_tanh(y):
    c1 = -2.0 * math.sqrt(2.0 / math.pi) * LOG2E
    c3 = c1 * 0.044715
    return y / (1.0 + jnp.exp2(y * (c1 + c3 * (y * y))))


def _inproj_kernel(x_ref, g_ref, sc_ref, sh_ref, w_ref, wf_ref, gq_ref, gk_ref,
                   cw_ref, cb_ref, wax_ref, bax_ref, lam_ref,
                   yrec_ref, q_ref, k_ref, v_ref, fl_ref,
                   xs_ref, gate_ref, hs_ref, hstate_ref, *, d_rec, d_att, n_heads):
    bsz, ts, d = x_ref.shape
    rows = bsz * ts
    n_slabs = d_rec // LANES
    grp = SUBLANES
    n_groups = bsz // grp
    grp_rows = ts * grp
    pad = SUBLANES
    hist = CONV_W - 1
    step = pl.program_id(0)

    @pl.when(step == 0)
    def _():
        xs_ref[:, :, 0:pad * grp, :] = jnp.zeros((n_slabs, n_groups, pad * grp, LANES), F32)
        hstate_ref[...] = jnp.zeros(hstate_ref.shape, F32)

    @pl.when(step > 0)
    def _():
        xs_ref[:, :, (pad - hist) * grp:pad * grp, :] = xs_ref[:, :, (pad + ts - hist) * grp:(pad + ts) * grp, :]

    x = x_ref[...]
    y = x * _rms_scale(x) * g_ref[...]
    h = (y * (1.0 + sc_ref[...]) + sh_ref[...]).reshape(rows, d).astype(BF16)

    xr = _dot(h, w_ref[:, 0:d_rec])
    for cs in range(n_slabs):
        for b in range(bsz):
            dst = xs_ref.at[cs, b // grp]
            dst[pl.ds(pad * grp + b % grp, ts, stride=grp), :] = xr[b * ts:(b + 1) * ts, cs * LANES:(cs + 1) * LANES]
    gate_ref[...] = _gelu_tanh(_dot(h, w_ref[:, d_rec:2 * d_rec]))
    fl_ref[...] = _dot(h, wf_ref[...])[:, :n_heads].reshape(bsz, ts, n_heads)

    lam = lam_ref[...]
    softplus_neg = jnp.maximum(-lam, 0.0) + jnp.log1p(jnp.exp(-jnp.abs(lam)))
    neg_log_a_gain = LRU_C * softplus_neg
    log2_a_gain = neg_log_a_gain * -LOG2E
    base = (pad - hist) * grp

    def recurrent_slab(cs):
        ln = slice(cs * GATE_BLOCK, (cs + 1) * GATE_BLOCK)
        u_groups = []
        for g_idx in range(n_groups):
            ug = xs_ref[cs, g_idx, base:base + grp_rows, :] * cw_ref[0:1, ln]
            for kk in range(1, CONV_W):
                ug = ug + xs_ref[cs, g_idx, base + kk * grp:base + kk * grp + grp_rows, :] * cw_ref[kk:kk + 1, ln]
            u_groups.append(ug)
        u = jnp.concatenate(u_groups, axis=0) + cb_ref[:, ln]
        g = _dot(u.astype(BF16), wax_ref[cs]) + bax_ref[cs]
        r = _sigmoid(g[:, :GATE_BLOCK])
        i = _sigmoid(g[:, GATE_BLOCK:])
        a = jnp.exp2(r * log2_a_gain[:, ln])
        one_minus_a2 = jnp.tanh(r * neg_log_a_gain[:, ln]) * (1.0 + a * a)
        mult = jnp.where(one_minus_a2 > 0.0, one_minus_a2 * lax.rsqrt(one_minus_a2), 0.0)
        b_in = mult * (i * u)
        hcur = [hstate_ref[g_idx * grp:(g_idx + 1) * grp, ln] for g_idx in range(n_groups)]
        hseq = [[] for _ in range(n_groups)]
        for tt in range(ts):
            for g_idx in range(n_groups):
                rs = slice(g_idx * grp_rows + tt * grp, g_idx * grp_rows + (tt + 1) * grp)
                hcur[g_idx] = a[rs] * hcur[g_idx] + b_in[rs]
                hseq[g_idx].append(hcur[g_idx])
        for g_idx in range(n_groups):
            hstate_ref[g_idx * grp:(g_idx + 1) * grp, ln] = hcur[g_idx]
            hs_ref[cs, g_idx] = jnp.concatenate(hseq[g_idx], axis=0)
        for b in range(bsz):
            hb = hs_ref.at[cs, b // grp][pl.ds(b % grp, ts, stride=grp), :]
            yrec_ref[b, :, ln] = (hb * gate_ref[b * ts:(b + 1) * ts, ln]).astype(BF16)

    def project_heads(dst_ref, col0, gain, heads):
        width = len(heads) * HEAD_DIM
        z = _dot(h, w_ref[:, col0 + heads[0] * HEAD_DIM:col0 + heads[0] * HEAD_DIM + width])
        for n, hd in enumerate(heads):
            zh = z[:, n * HEAD_DIM:(n + 1) * HEAD_DIM]
            if gain is not None:
                zh = zh * _rms_scale(zh) * gain
            dst_ref[:, hd] = zh.astype(BF16).reshape(bsz, ts, HEAD_DIM)

    off = 2 * d_rec
    q_gain = gq_ref[...] * (HEAD_DIM ** -0.5 * LOG2E)
    k_gain = gk_ref[...]
    head_groups = [list(range(g0, g0 + HEADS_PER_DOT)) for g0 in range(0, n_heads, HEADS_PER_DOT)]
    dots = ([(q_ref, off, q_gain, hg) for hg in head_groups]
            + [(k_ref, off + d_att, k_gain, hg) for hg in head_groups]
            + [(v_ref, off + 2 * d_att, None, hg) for hg in head_groups])
    slabs = list(range(n_slabs))
    n_dots = len(dots)
    for n, dot_args in enumerate(dots):
        project_heads(*dot_args)
        while slabs and (n_slabs - len(slabs)) * n_dots < (n + 1) * n_slabs:
            recurrent_slab(slabs.pop(0))


def _inproj(x, g_mix, sc1, sh1, w_main, w_f, g_q, g_k, conv_w, conv_b, wax, bax, lam,
            *, d_rec, d_att, n_heads):
    bsz, seq, d = x.shape
    ts = TOKEN_TS
    n_slabs = d_rec // LANES
    const = lambda i: (0, 0)
    const3 = lambda i: (0, 0, 0)
    tchunk = lambda i: (0, i, 0)
    head_shape = jax.ShapeDtypeStruct((bsz, n_heads, seq, HEAD_DIM), BF16)
    head_spec = pl.BlockSpec((bsz, n_heads, ts, HEAD_DIM), lambda i: (0, 0, i, 0))
    return pl.pallas_call(
        functools.partial(_inproj_kernel, d_rec=d_rec, d_att=d_att, n_heads=n_heads),
        grid=(seq // ts,),
        in_specs=[
            pl.BlockSpec((bsz, ts, d), tchunk),
            pl.BlockSpec((1, d), const),
            pl.BlockSpec((bsz, 1, d), const3),
            pl.BlockSpec((bsz, 1, d), const3),
            pl.BlockSpec((d, 2 * d_rec + 3 * d_att), const, pipeline_mode=pl.Buffered(1)),
            pl.BlockSpec(w_f.shape, const, pipeline_mode=pl.Buffered(1)),
            pl.BlockSpec((1, HEAD_DIM), const),
            pl.BlockSpec((1, HEAD_DIM), const),
            pl.BlockSpec(conv_w.shape, const),
            pl.BlockSpec(conv_b.shape, const),
            pl.BlockSpec(wax.shape, const3),
            pl.BlockSpec(bax.shape, const3),
            pl.BlockSpec(lam.shape, const),
        ],
        out_specs=[
            pl.BlockSpec((bsz, ts, d_rec), tchunk),
            head_spec, head_spec, head_spec,
            pl.BlockSpec((bsz, ts, n_heads), tchunk),
        ],
        out_shape=[
            jax.ShapeDtypeStruct((bsz, seq, d_rec), BF16),
            head_shape, head_shape, head_shape,
            jax.ShapeDtypeStruct((bsz, seq, n_heads), F32),
        ],
        scratch_shapes=[
            pltpu.VMEM((n_slabs, bsz // SUBLANES, (SUBLANES + ts) * SUBLANES, LANES), F32),
            pltpu.VMEM((bsz * ts, d_rec), F32),
            pltpu.VMEM((n_slabs, bsz // SUBLANES, ts * SUBLANES, LANES), F32),
            pltpu.VMEM((bsz, d_rec), F32),
        ],
        compiler_params=_params(1),
        name="inproj",
    )(x, g_mix, sc1, sh1, w_main, w_f, g_q, g_k, conv_w, conv_b, wax, bax, lam)


def _cum_kernel(f_ref, b_ref, o_ref):
    z = f_ref[...] + b_ref[...]
    x = -(jnp.maximum(-z, 0.0) + jnp.log1p(jnp.exp(-jnp.abs(z))))
    n = x.shape[-1]
    lane = lax.broadcasted_iota(jnp.int32, x.shape, 1)
    d = 1
    while d < n:
        x = x + jnp.where(lane >= d, pltpu.roll(x, d, axis=1), 0.0)
        d *= 2
    o_ref[...] = x


def _cum(f_rows, b_rows):
    r, n = f_rows.shape
    return pl.pallas_call(
        _cum_kernel,
        grid=(1,),
        in_specs=[pl.BlockSpec((r, n), lambda i: (0, 0)),
                  pl.BlockSpec((r, 1), lambda i: (0, 0))],
        out_specs=pl.BlockSpec((r, n), lambda i: (0, 0)),
        out_shape=jax.ShapeDtypeStruct((r, n), F32),
        compiler_params=_params(1),
        name="cum",
    )(f_rows, b_rows)


def _attn_kernel(q_ref, k_ref, v_ref, cum_ref, o_ref):
    seq = q_ref.shape[2]
    t = ATTN_T
    n_blocks = seq // t
    row = lax.broadcasted_iota(jnp.int32, (t, t), 0)
    col = lax.broadcasted_iota(jnp.int32, (t, t), 1)
    causal = col <= row
    ones_cols = jnp.ones((t, HEAD_DIM), BF16)

    n_h = q_ref.shape[1]
    m = [None] * n_h
    acc = [None] * n_h
    for kj in range(n_blocks):
        blk = slice(kj * t, (kj + 1) * t)
        last = kj == n_blocks - 1
        for hh in range(n_h):
            q = q_ref[0, hh, kj * t:, :]
            kb = k_ref[0, hh, blk, :]
            vb = jnp.concatenate([v_ref[0, hh, blk, :], ones_cols], axis=1)
            s = lax.dot_general(q, kb, (((1,), (1,)), ((), ())),
                                preferred_element_type=F32) - cum_ref[0, hh, kj] * LOG2E
            s_diag = jnp.where(causal, s[:t], -jnp.inf)
            s = s_diag if last else jnp.concatenate([s_diag, s[t:]], axis=0)
            m_blk = jnp.max(s, axis=1, keepdims=True)
            if kj == 0:
                m_new = m_blk
                acc_new = _dot(jnp.exp2(s - m_new).astype(BF16), vb)
            else:
                m_new = jnp.maximum(m[hh], m_blk)
                acc_new = jnp.exp2(m[hh] - m_new) * acc[hh] + _dot(jnp.exp2(s - m_new).astype(BF16), vb)
            o_ref[0, hh, blk, :] = (acc_new[:t, :HEAD_DIM] / acc_new[:t, HEAD_DIM:]).astype(BF16)
            if not last:
                m[hh], acc[hh] = m_new[t:], acc_new[t:]


def _attn(q, k, v, cum):
    bsz, n_heads, seq, dh = q.shape
    cum5 = cum.reshape(bsz, n_heads, seq // ATTN_T, 1, ATTN_T)
    hps = ATTN_HEADS_PER_STEP
    spec = pl.BlockSpec((1, hps, seq, dh), lambda b, h: (b, h, 0, 0))
    return pl.pallas_call(
        _attn_kernel,
        grid=(bsz, n_heads // hps),
        in_specs=[spec, spec, spec,
                  pl.BlockSpec((1, hps) + cum5.shape[2:], lambda b, h: (b, h, 0, 0, 0))],
        out_specs=spec,
        out_shape=jax.ShapeDtypeStruct(q.shape, BF16),
        compiler_params=_params(2),
        name="attn",
    )(q, k, v, cum5)


def _outproj_kernel(yrec_ref, yatt_ref, x_ref, w_ref, gor_ref, goa_ref, gt_ref,
                    gf_ref, sc_ref, sh_ref, x1_ref, h2_ref, *, n_heads):
    bsz, ts, d = x_ref.shape
    yr = yrec_ref[...].astype(F32).reshape(bsz * ts, yrec_ref.shape[2])
    parts = [(yr * _rms_scale(yr) * gor_ref[...]).astype(BF16)]
    heads = [yatt_ref[:, hd].astype(F32).reshape(bsz * ts, HEAD_DIM) for hd in range(n_heads)]
    ssq = heads[0] * heads[0]
    for hh in heads[1:]:
        ssq = ssq + hh * hh
    inv = lax.rsqrt(jnp.sum(ssq, axis=-1, keepdims=True) / (n_heads * HEAD_DIM) + EPS)
    for hd in range(n_heads):
        gain = goa_ref[:, hd * HEAD_DIM:(hd + 1) * HEAD_DIM]
        parts.append((heads[hd] * inv * gain).astype(BF16))
    mix = jnp.concatenate(parts, axis=1)
    x1 = x_ref[...] + gt_ref[...] * _dot(mix, w_ref[...]).reshape(bsz, ts, d)
    x1_ref[...] = x1
    y = x1 * _rms_scale(x1) * gf_ref[...]
    h2_ref[...] = (y * (1.0 + sc_ref[...]) + sh_ref[...]).astype(BF16)


def _outproj(y_rec, y_att, x, w_out, g_or, g_oa, gt1, g_ffn, sc2, sh2):
    bsz, seq, d = x.shape
    _, n_heads, _, dh = y_att.shape
    d_rec = y_rec.shape[2]
    ts = TOKEN_TS
    const = lambda i: (0, 0)
    const3 = lambda i: (0, 0, 0)
    tchunk = lambda i: (0, i, 0)
    tok_spec = pl.BlockSpec((bsz, ts, d), tchunk)
    mod_spec = pl.BlockSpec((bsz, 1, d), const3)
    return pl.pallas_call(
        functools.partial(_outproj_kernel, n_heads=n_heads),
        grid=(seq // ts,),
        in_specs=[
            pl.BlockSpec((bsz, ts, d_rec), tchunk),
            pl.BlockSpec((bsz, n_heads, ts, dh), lambda i: (0, 0, i, 0)),
            tok_spec,
            pl.BlockSpec(w_out.shape, const, pipeline_mode=pl.Buffered(1)),
            pl.BlockSpec((1, d_rec), const),
            pl.BlockSpec((1, n_heads * dh), const),
            mod_spec,
            pl.BlockSpec((1, d), const),
            mod_spec,
            mod_spec,
        ],
        out_specs=[tok_spec, tok_spec],
        out_shape=[jax.ShapeDtypeStruct(x.shape, F32), jax.ShapeDtypeStruct(x.shape, BF16)],
        compiler_params=_params(1),
        name="outproj",
    )(y_rec, y_att, x, w_out, g_or, g_oa, gt1, g_ffn, sc2, sh2)


def _up_kernel(h_ref, wg_ref, wu_ref, a_ref, wg_b, wu_b):
    @pl.when(pl.program_id(1) == 0)
    def _():
        wg_b[...] = wg_ref[...].astype(BF16)
        wu_b[...] = wu_ref[...].astype(BF16)

    h = h_ref[...]
    g = _dot(h, wg_b[...])
    u = _dot(h, wu_b[...])
    a_ref[...] = (g * jax.nn.sigmoid(g) * u).astype(BF16)


def _up(h2, w_up, d_ff):
    t, d = h2.shape
    tm, tn = UP_TM, UP_TN
    nj = d_ff // tn
    return pl.pallas_call(
        _up_kernel,
        grid=(nj, t // tm),
        in_specs=[
            pl.BlockSpec((tm, d), lambda j, i: (i, 0)),
            pl.BlockSpec((d, tn), lambda j, i: (0, j)),
            pl.BlockSpec((d, tn), lambda j, i: (0, j + nj)),
        ],
        out_specs=pl.BlockSpec((tm, tn), lambda j, i: (i, j)),
        out_shape=jax.ShapeDtypeStruct((t, d_ff), BF16),
        scratch_shapes=[pltpu.VMEM((d, tn), BF16), pltpu.VMEM((d, tn), BF16)],
        compiler_params=_params(2),
        name="up",
    )(h2, w_up, w_up)


def _down_kernel(a_ref, w_ref, x1_ref, gt_ref, o_ref):
    o_ref[...] = x1_ref[...] + gt_ref[0] * _dot(a_ref[...], w_ref[...])


def _down(a, w_down, x1, gt2, *, seq):
    t, d_ff = a.shape
    d = w_down.shape[1]
    tm, tn = DOWN_TM, DOWN_TN
    spb = seq // tm
    return pl.pallas_call(
        _down_kernel,
        grid=(t // tm, d // tn),
        in_specs=[
            pl.BlockSpec((tm, d_ff), lambda i, j: (i, 0)),
            pl.BlockSpec((d_ff, tn), lambda i, j: (0, j)),
            pl.BlockSpec((tm, tn), lambda i, j: (i, j)),
            pl.BlockSpec((1, 1, tn), lambda i, j: (i // spb, 0, j)),
        ],
        out_specs=pl.BlockSpec((tm, tn), lambda i, j: (i, j)),
        out_shape=jax.ShapeDtypeStruct((t, d), F32),
        compiler_params=_params(2),
        name="down",
    )(a, w_down, x1, gt2)


def kernel(x, c, w_ada, b_ada, g_mix, w_in, conv_w, conv_b, w_gate_a, b_gate_a, w_gate_x, b_gate_x,
           lru_logit, b_forget, g_q, g_k, g_out_rec, g_out_att, w_out, g_ffn, w_up, w_down):
    bsz, seq, d = x.shape
    depth = w_ada.shape[0]
    d_rec = conv_w.shape[-1]
    n_heads = b_forget.shape[-1]
    d_att = n_heads * HEAD_DIM
    n_gate_blocks = w_gate_a.shape[1]
    d_ff = w_down.shape[1]
    t = bsz * seq
    row = lambda v: v.reshape(1, -1)

    for l in range(depth):
        mod = _mod(c, w_ada[l], b_ada[l])
        sh1, sc1, gt1, sh2, sc2, gt2 = [m.reshape(bsz, 1, d) for m in jnp.split(mod, 6, axis=-1)]

        w_in_b = w_in[l].astype(BF16)
        w_f = jnp.pad(w_in_b[:, 2 * d_rec + 3 * d_att:], ((0, 0), (0, LANES - n_heads)))
        wax = jnp.concatenate([w_gate_a[l], w_gate_x[l]], axis=-1).astype(BF16)
        bax = jnp.concatenate([b_gate_a[l].reshape(n_gate_blocks, 1, GATE_BLOCK),
                               b_gate_x[l].reshape(n_gate_blocks, 1, GATE_BLOCK)], axis=-1)
        y_rec, q, k, v, fl = _inproj(x, row(g_mix[l]), sc1, sh1, w_in_b, w_f, row(g_q[l]), row(g_k[l]),
                                     conv_w[l], row(conv_b[l]), wax, bax, row(lru_logit[l]),
                                     d_rec=d_rec, d_att=d_att, n_heads=n_heads)

        f_rows = fl.transpose(0, 2, 1).reshape(bsz * n_heads, seq)
        b_rows = jnp.tile(b_forget[l], bsz).reshape(bsz * n_heads, 1)
        cum = _cum(f_rows, b_rows)
        y_att = _attn(q, k, v, cum)

        x1, h2 = _outproj(y_rec, y_att, x, w_out[l].astype(BF16), row(g_out_rec[l]), row(g_out_att[l]),
                          gt1, row(g_ffn[l]), sc2, sh2)
        a = _up(h2.reshape(t, d), w_up[l], d_ff)
        x = _down(a, w_down[l].astype(BF16), x1.reshape(t, d), gt2, seq=seq).reshape(bsz, seq, d)
    return x
```

```python
import functools
import math

import jax
import jax.numpy as jnp
from jax import lax
from jax.experimental import pallas as pl
from jax.experimental.pallas import tpu as pltpu

F32 = jnp.float32
BF16 = jnp.bfloat16

EPS = 1e-6
LRU_C = 8.0
CONV_W = 4
HEAD_DIM = 128
GATE_BLOCK = 128
LOG2E = math.log2(math.e)

LANES = 128
SUBLANES = 8
VMEM_LIMIT_BYTES = 56 * 1024 * 1024

MOD_TN = 1024
TOKEN_TS = 32
HEADS_PER_DOT = 8
ATTN_T = 256
ATTN_HEADS_PER_STEP = 4
UP_TM, UP_TN = 2048, 512
UP_ROWS_PER_DOT = 1024
DOWN_TM, DOWN_TN = 1024, 512


def _params(n_axes):
    return pltpu.CompilerParams(
        dimension_semantics=("arbitrary",) * n_axes,
        vmem_limit_bytes=VMEM_LIMIT_BYTES,
    )


def _dot(a, b):
    return jnp.dot(a, b, preferred_element_type=F32)


def _rms_scale(x):
    return lax.rsqrt(jnp.mean(x * x, axis=-1, keepdims=True) + EPS)


def _mod_kernel(c_ref, w_ref, b_ref, o_ref):
    c = c_ref[...]
    ca = (c * jax.nn.sigmoid(c)).astype(BF16)
    o_ref[...] = _dot(ca, w_ref[...].astype(BF16)) + b_ref[...]


def _mod(c, w_ada, b_ada):
    bsz, d = c.shape
    n = w_ada.shape[1]
    return pl.pallas_call(
        _mod_kernel,
        grid=(n // MOD_TN,),
        in_specs=[
            pl.BlockSpec((bsz, d), lambda j: (0, 0)),
            pl.BlockSpec((d, MOD_TN), lambda j: (0, j)),
            pl.BlockSpec((1, MOD_TN), lambda j: (0, j)),
        ],
        out_specs=pl.BlockSpec((bsz, MOD_TN), lambda j: (0, j)),
        out_shape=jax.ShapeDtypeStruct((bsz, n), F32),
        compiler_params=_params(1),
        name="mod",
    )(c, w_ada, b_ada.reshape(1, n))


def _sigmoid(x):
    return 1.0 / (1.0 + jnp.exp2(x * -LOG2E))


def _gelu_tanh(y):
    c1 = -2.0 * math.sqrt(2.0 / math.pi) * LOG2E
    c3 = c1 * 0.044715
    return y / (1.0 + jnp.exp2(y * (c1 + c3 * (y * y))))


def _inproj_kernel(x_ref, g_ref, sc_ref, sh_ref, w_ref, wf_ref, gq_ref, gk_ref,
                   cw_ref, cb_ref, wax_ref, bax_ref, lam_ref,
                   yrec_ref, q_ref, k_ref, v_ref, fl_ref,
                   xs_ref, gate_ref, hs_ref, hstate_ref, *, d_rec, d_att, n_heads):
    bsz, ts, d = x_ref.shape
    rows = bsz * ts
    n_slabs = d_rec // LANES
    grp = SUBLANES
    n_groups = bsz // grp
    grp_rows = ts * grp
    pad = SUBLANES
    hist = CONV_W - 1
    step = pl.program_id(0)

    @pl.when(step == 0)
    def _():
        xs_ref[:, :, 0:pad * grp, :] = jnp.zeros((n_slabs, n_groups, pad * grp, LANES), F32)
        hstate_ref[...] = jnp.zeros(hstate_ref.shape, F32)

    @pl.when(step > 0)
    def _():
        xs_ref[:, :, (pad - hist) * grp:pad * grp, :] = xs_ref[:, :, (pad + ts - hist) * grp:(pad + ts) * grp, :]

    x = x_ref[...]
    y = x * _rms_scale(x) * g_ref[...]
    h = (y * (1.0 + sc_ref[...]) + sh_ref[...]).reshape(rows, d).astype(BF16)

    xr = _dot(h, w_ref[:, 0:d_rec])
    for cs in range(n_slabs):
        for b in range(bsz):
            dst = xs_ref.at[cs, b // grp]
            dst[pl.ds(pad * grp + b % grp, ts, stride=grp), :] = xr[b * ts:(b + 1) * ts, cs * LANES:(cs + 1) * LANES]
    gate_ref[...] = _gelu_tanh(_dot(h, w_ref[:, d_rec:2 * d_rec]))
    fl_ref[...] = _dot(h, wf_ref[...])[:, :n_heads].reshape(bsz, ts, n_heads)

    lam = lam_ref[...]
    softplus_neg = jnp.maximum(-lam, 0.0) + jnp.log1p(jnp.exp(-jnp.abs(lam)))
    neg_log_a_gain = LRU_C * softplus_neg
    log2_a_gain = neg_log_a_gain * -LOG2E
    base = (pad - hist) * grp

    def recurrent_slab(cs):
        ln = slice(cs * GATE_BLOCK, (cs + 1) * GATE_BLOCK)
        u_groups = []
        for g_idx in range(n_groups):
            ug = xs_ref[cs, g_idx, base:base + grp_rows, :] * cw_ref[0:1, ln]
            for kk in range(1, CONV_W):
                ug = ug + xs_ref[cs, g_idx, base + kk * grp:base + kk * grp + grp_rows, :] * cw_ref[kk:kk + 1, ln]
            u_groups.append(ug)
        u = jnp.concatenate(u_groups, axis=0) + cb_ref[:, ln]
        g = _dot(u.astype(BF16), wax_ref[cs]) + bax_ref[cs]
        r = _sigmoid(g[:, :GATE_BLOCK])
        i = _sigmoid(g[:, GATE_BLOCK:])
        a = jnp.exp2(r * log2_a_gain[:, ln])
        one_minus_a2 = jnp.tanh(r * neg_log_a_gain[:, ln]) * (1.0 + a * a)
        mult = jnp.where(one_minus_a2 > 0.0, one_minus_a2 * lax.rsqrt(one_minus_a2), 0.0)
        b_in = mult * (i * u)
        hcur = [hstate_ref[g_idx * grp:(g_idx + 1) * grp, ln] for g_idx in range(n_groups)]
        hseq = [[] for _ in range(n_groups)]
        for tt in range(ts):
            for g_idx in range(n_groups):
                rs = slice(g_idx * grp_rows + tt * grp, g_idx * grp_rows + (tt + 1) * grp)
                hcur[g_idx] = a[rs] * hcur[g_idx] + b_in[rs]
                hseq[g_idx].append(hcur[g_idx])
        for g_idx in range(n_groups):
            hstate_ref[g_idx * grp:(g_idx + 1) * grp, ln] = hcur[g_idx]
            hs_ref[cs, g_idx] = jnp.concatenate(hseq[g_idx], axis=0)
        for b in range(bsz):
            hb = hs_ref.at[cs, b // grp][pl.ds(b % grp, ts, stride=grp), :]
            yrec_ref[b, :, ln] = (hb * gate_ref[b * ts:(b + 1) * ts, ln]).astype(BF16)

    def project_heads(dst_ref, col0, gain, heads):
        width = len(heads) * HEAD_DIM
        z = _dot(h, w_ref[:, col0 + heads[0] * HEAD_DIM:col0 + heads[0] * HEAD_DIM + width])
        for n, hd in enumerate(heads):
            zh = z[:, n * HEAD_DIM:(n + 1) * HEAD_DIM]
            if gain is not None:
                zh = zh * _rms_scale(zh) * gain
            dst_ref[:, hd] = zh.astype(BF16).reshape(bsz, ts, HEAD_DIM)

    off = 2 * d_rec
    q_gain = gq_ref[...] * (HEAD_DIM ** -0.5 * LOG2E)
    k_gain = gk_ref[...]
    head_groups = [list(range(g0, g0 + HEADS_PER_DOT)) for g0 in range(0, n_heads, HEADS_PER_DOT)]
    dots = ([(q_ref, off, q_gain, hg) for hg in head_groups]
            + [(k_ref, off + d_att, k_gain, hg) for hg in head_groups]
            + [(v_ref, off + 2 * d_att, None, hg) for hg in head_groups])
    slabs = list(range(n_slabs))
    n_dots = len(dots)
    for n, dot_args in enumerate(dots):
        project_heads(*dot_args)
        while slabs and (n_slabs - len(slabs)) * n_dots < (n + 1) * n_slabs:
            recurrent_slab(slabs.pop(0))


def _inproj(x, g_mix, sc1, sh1, w_main, w_f, g_q, g_k, conv_w, conv_b, wax, bax, lam,
            *, d_rec, d_att, n_heads):
    bsz, seq, d = x.shape
    ts = TOKEN_TS
    n_slabs = d_rec // LANES
    const = lambda i: (0, 0)
    const3 = lambda i: (0, 0, 0)
    tchunk = lambda i: (0, i, 0)
    head_shape = jax.ShapeDtypeStruct((bsz, n_heads, seq, HEAD_DIM), BF16)
    head_spec = pl.BlockSpec((bsz, n_heads, ts, HEAD_DIM), lambda i: (0, 0, i, 0))
    return pl.pallas_call(
        functools.partial(_inproj_kernel, d_rec=d_rec, d_att=d_att, n_heads=n_heads),
        grid=(seq // ts,),
        in_specs=[
            pl.BlockSpec((bsz, ts, d), tchunk),
            pl.BlockSpec((1, d), const),
            pl.BlockSpec((bsz, 1, d), const3),
            pl.BlockSpec((bsz, 1, d), const3),
            pl.BlockSpec((d, 2 * d_rec + 3 * d_att), const, pipeline_mode=pl.Buffered(1)),
            pl.BlockSpec(w_f.shape, const, pipeline_mode=pl.Buffered(1)),
            pl.BlockSpec((1, HEAD_DIM), const),
            pl.BlockSpec((1, HEAD_DIM), const),
            pl.BlockSpec(conv_w.shape, const),
            pl.BlockSpec(conv_b.shape, const),
            pl.BlockSpec(wax.shape, const3),
            pl.BlockSpec(bax.shape, const3),
            pl.BlockSpec(lam.shape, const),
        ],
        out_specs=[
            pl.BlockSpec((bsz, ts, d_rec), tchunk),
            head_spec, head_spec, head_spec,
            pl.BlockSpec((bsz, ts, n_heads), tchunk),
        ],
        out_shape=[
            jax.ShapeDtypeStruct((bsz, seq, d_rec), BF16),
            head_shape, head_shape, head_shape,
            jax.ShapeDtypeStruct((bsz, seq, n_heads), F32),
        ],
        scratch_shapes=[
            pltpu.VMEM((n_slabs, bsz // SUBLANES, (SUBLANES + ts) * SUBLANES, LANES), F32),
            pltpu.VMEM((bsz * ts, d_rec), F32),
            pltpu.VMEM((n_slabs, bsz // SUBLANES, ts * SUBLANES, LANES), F32),
            pltpu.VMEM((bsz, d_rec), F32),
        ],
        compiler_params=_params(1),
        name="inproj",
    )(x, g_mix, sc1, sh1, w_main, w_f, g_q, g_k, conv_w, conv_b, wax, bax, lam)


def _cum_kernel(f_ref, b_ref, o_ref):
    z = f_ref[...] + b_ref[...]
    x = -(jnp.maximum(-z, 0.0) + jnp.log1p(jnp.exp(-jnp.abs(z))))
    n = x.shape[-1]
    lane = lax.broadcasted_iota(jnp.int32, x.shape, 1)
    d = 1
    while d < n:
        x = x + jnp.where(lane >= d, pltpu.roll(x, d, axis=1), 0.0)
        d *= 2
    o_ref[...] = x


def _cum(f_rows, b_rows):
    r, n = f_rows.shape
    return pl.pallas_call(
        _cum_kernel,
        grid=(1,),
        in_specs=[pl.BlockSpec((r, n), lambda i: (0, 0)),
                  pl.BlockSpec((r, 1), lambda i: (0, 0))],
        out_specs=pl.BlockSpec((r, n), lambda i: (0, 0)),
        out_shape=jax.ShapeDtypeStruct((r, n), F32),
        compiler_params=_params(1),
        name="cum",
    )(f_rows, b_rows)


def _attn_kernel(q_ref, k_ref, v_ref, cum_ref, o_ref):
    seq = q_ref.shape[2]
    t = ATTN_T
    n_blocks = seq // t
    row = lax.broadcasted_iota(jnp.int32, (t, t), 0)
    col = lax.broadcasted_iota(jnp.int32, (t, t), 1)
    causal = col <= row
    ones_cols = jnp.ones((t, HEAD_DIM), BF16)

    n_h = q_ref.shape[1]
    m = [None] * n_h
    acc = [None] * n_h
    for kj in range(n_blocks):
        blk = slice(kj * t, (kj + 1) * t)
        last = kj == n_blocks - 1
        for hh in range(n_h):
            q = q_ref[0, hh, kj * t:, :]
            kb = k_ref[0, hh, blk, :]
            vb = jnp.concatenate([v_ref[0, hh, blk, :], ones_cols], axis=1)
            s = lax.dot_general(q, kb, (((1,), (1,)), ((), ())),
                                preferred_element_type=F32) - cum_ref[0, hh, kj] * LOG2E
            s_diag = jnp.where(causal, s[:t], -jnp.inf)
            s = s_diag if last else jnp.concatenate([s_diag, s[t:]], axis=0)
            m_blk = jnp.max(s, axis=1, keepdims=True)
            if kj == 0:
                m_new = m_blk
                acc_new = _dot(jnp.exp2(s - m_new).astype(BF16), vb)
            else:
                m_new = jnp.maximum(m[hh], m_blk)
                acc_new = jnp.exp2(m[hh] - m_new) * acc[hh] + _dot(jnp.exp2(s - m_new).astype(BF16), vb)
            o_ref[0, hh, blk, :] = (acc_new[:t, :HEAD_DIM] / acc_new[:t, HEAD_DIM:]).astype(BF16)
            if not last:
                m[hh], acc[hh] = m_new[t:], acc_new[t:]


def _attn(q, k, v, cum):
    bsz, n_heads, seq, dh = q.shape
    cum5 = cum.reshape(bsz, n_heads, seq // ATTN_T, 1, ATTN_T)
    hps = ATTN_HEADS_PER_STEP
    spec = pl.BlockSpec((1, hps, seq, dh), lambda b, h: (b, h, 0, 0))
    return pl.pallas_call(
        _attn_kernel,
        grid=(bsz, n_heads // hps),
        in_specs=[spec, spec, spec,
                  pl.BlockSpec((1, hps) + cum5.shape[2:], lambda b, h: (b, h, 0, 0, 0))],
        out_specs=spec,
        out_shape=jax.ShapeDtypeStruct(q.shape, BF16),
        compiler_params=_params(2),
        name="attn",
    )(q, k, v, cum5)


def _outproj_kernel(yrec_ref, yatt_ref, x_ref, w_ref, gor_ref, goa_ref, gt_ref,
                    gf_ref, sc_ref, sh_ref, x1_ref, h2_ref, *, n_heads):
    bsz, ts, d = x_ref.shape
    yr = yrec_ref[...].astype(F32).reshape(bsz * ts, yrec_ref.shape[2])
    parts = [(yr * _rms_scale(yr) * gor_ref[...]).astype(BF16)]
    heads = [yatt_ref[:, hd].astype(F32).reshape(bsz * ts, HEAD_DIM) for hd in range(n_heads)]
    ssq = heads[0] * heads[0]
    for hh in heads[1:]:
        ssq = ssq + hh * hh
    inv = lax.rsqrt(jnp.sum(ssq, axis=-1, keepdims=True) / (n_heads * HEAD_DIM) + EPS)
    for hd in range(n_heads):
        gain = goa_ref[:, hd * HEAD_DIM:(hd + 1) * HEAD_DIM]
        parts.append((heads[hd] * inv * gain).astype(BF16))
    mix = jnp.concatenate(parts, axis=1)
    x1 = x_ref[...] + gt_ref[...] * _dot(mix, w_ref[...]).reshape(bsz, ts, d)
    x1_ref[...] = x1
    y = x1 * _rms_scale(x1) * gf_ref[...]
    h2_ref[...] = (y * (1.0 + sc_ref[...]) + sh_ref[...]).astype(BF16)


def _outproj(y_rec, y_att, x, w_out, g_or, g_oa, gt1, g_ffn, sc2, sh2):
    bsz, seq, d = x.shape
    _, n_heads, _, dh = y_att.shape
    d_rec = y_rec.shape[2]
    ts = TOKEN_TS
    const = lambda i: (0, 0)
    const3 = lambda i: (0, 0, 0)
    tchunk = lambda i: (0, i, 0)
    tok_spec = pl.BlockSpec((bsz, ts, d), tchunk)
    mod_spec = pl.BlockSpec((bsz, 1, d), const3)
    return pl.pallas_call(
        functools.partial(_outproj_kernel, n_heads=n_heads),
        grid=(seq // ts,),
        in_specs=[
            pl.BlockSpec((bsz, ts, d_rec), tchunk),
            pl.BlockSpec((bsz, n_heads, ts, dh), lambda i: (0, 0, i, 0)),
            tok_spec,
            pl.BlockSpec(w_out.shape, const, pipeline_mode=pl.Buffered(1)),
            pl.BlockSpec((1, d_rec), const),
            pl.BlockSpec((1, n_heads * dh), const),
            mod_spec,
            pl.BlockSpec((1, d), const),
            mod_spec,
            mod_spec,
        ],
        out_specs=[tok_spec, tok_spec],
        out_shape=[jax.ShapeDtypeStruct(x.shape, F32), jax.ShapeDtypeStruct(x.shape, BF16)],
        compiler_params=_params(1),
        name="outproj",
    )(y_rec, y_att, x, w_out, g_or, g_oa, gt1, g_ffn, sc2, sh2)


def _up_kernel(h_ref, wg_ref, wu_ref, a_ref, wg_b, wu_b):
    @pl.when(pl.program_id(1) == 0)
    def _():
        wg_b[...] = wg_ref[...].astype(BF16)
        wu_b[...] = wu_ref[...].astype(BF16)

    for r0 in range(0, h_ref.shape[0], UP_ROWS_PER_DOT):
        rows = slice(r0, r0 + UP_ROWS_PER_DOT)
        h = h_ref[rows, :]
        g = _dot(h, wg_b[...])
        u = _dot(h, wu_b[...])
        a_ref[rows, :] = (g * jax.nn.sigmoid(g) * u).astype(BF16)


def _up(h2, w_up, d_ff):
    t, d = h2.shape
    tm, tn = UP_TM, UP_TN
    nj = d_ff // tn
    return pl.pallas_call(
        _up_kernel,
        grid=(nj, t // tm),
        in_specs=[
            pl.BlockSpec((tm, d), lambda j, i: (i, 0)),
            pl.BlockSpec((d, tn), lambda j, i: (0, j)),
            pl.BlockSpec((d, tn), lambda j, i: (0, j + nj)),
        ],
        out_specs=pl.BlockSpec((tm, tn), lambda j, i: (i, j)),
        out_shape=jax.ShapeDtypeStruct((t, d_ff), BF16),
        scratch_shapes=[pltpu.VMEM((d, tn), BF16), pltpu.VMEM((d, tn), BF16)],
        compiler_params=_params(2),
        name="up",
    )(h2, w_up, w_up)


def _down_kernel(a_ref, w_ref, x1_ref, gt_ref, o_ref):
    o_ref[...] = x1_ref[...] + gt_ref[0] * _dot(a_ref[...], w_ref[...])


def _down(a, w_down, x1, gt2, *, seq):
    t, d_ff = a.shape
    d = w_down.shape[1]
    tm, tn = DOWN_TM, DOWN_TN
    spb = seq // tm
    return pl.pallas_call(
        _down_kernel,
        grid=(t // tm, d // tn),
        in_specs=[
            pl.BlockSpec((tm, d_ff), lambda i, j: (i, 0)),
            pl.BlockSpec((d_ff, tn), lambda i, j: (0, j)),
            pl.BlockSpec((tm, tn), lambda i, j: (i, j)),
            pl.BlockSpec((1, 1, tn), lambda i, j: (i // spb, 0, j)),
        ],
        out_specs=pl.BlockSpec((tm, tn), lambda i, j: (i, j)),
        out_shape=jax.ShapeDtypeStruct((t, d), F32),
        compiler_params=_params(2),
        name="down",
    )(a, w_down, x1, gt2)


def kernel(x, c, w_ada, b_ada, g_mix, w_in, conv_w, conv_b, w_gate_a, b_gate_a, w_gate_x, b_gate_x,
           lru_logit, b_forget, g_q, g_k, g_out_rec, g_out_att, w_out, g_ffn, w_up, w_down):
    bsz, seq, d = x.shape
    depth = w_ada.shape[0]
    d_rec = conv_w.shape[-1]
    n_heads = b_forget.shape[-1]
    d_att = n_heads * HEAD_DIM
    n_gate_blocks = w_gate_a.shape[1]
    d_ff = w_down.shape[1]
    t = bsz * seq
    row = lambda v: v.reshape(1, -1)

    for l in range(depth):
        mod = _mod(c, w_ada[l], b_ada[l])
        sh1, sc1, gt1, sh2, sc2, gt2 = [m.reshape(bsz, 1, d) for m in jnp.split(mod, 6, axis=-1)]

        w_in_b = w_in[l].astype(BF16)
        w_f = jnp.pad(w_in_b[:, 2 * d_rec + 3 * d_att:], ((0, 0), (0, LANES - n_heads)))
        wax = jnp.concatenate([w_gate_a[l], w_gate_x[l]], axis=-1).astype(BF16)
        bax = jnp.concatenate([b_gate_a[l].reshape(n_gate_blocks, 1, GATE_BLOCK),
                               b_gate_x[l].reshape(n_gate_blocks, 1, GATE_BLOCK)], axis=-1)
        y_rec, q, k, v, fl = _inproj(x, row(g_mix[l]), sc1, sh1, w_in_b, w_f, row(g_q[l]), row(g_k[l]),
                                     conv_w[l], row(conv_b[l]), wax, bax, row(lru_logit[l]),
                                     d_rec=d_rec, d_att=d_att, n_heads=n_heads)

        f_rows = fl.transpose(0, 2, 1).reshape(bsz * n_heads, seq)
        b_rows = jnp.tile(b_forget[l], bsz).reshape(bsz * n_heads, 1)
        cum = _cum(f_rows, b_rows)
        y_att = _attn(q, k, v, cum)

        x1, h2 = _outproj(y_rec, y_att, x, w_out[l].astype(BF16), row(g_out_rec[l]), row(g_out_att[l]),
                          gt1, row(g_ffn[l]), sc2, sh2)
        a = _up(h2.reshape(t, d), w_up[l], d_ff)
        x = _down(a, w_down[l].astype(BF16), x1.reshape(t, d), gt2, seq=seq).reshape(bsz, seq, d)
    return x
```

```python
import functools
import math

import jax
import jax.numpy as jnp
from jax import lax
from jax.experimental import pallas as pl
from jax.experimental.pallas import tpu as pltpu

F32 = jnp.float32
BF16 = jnp.bfloat16

EPS = 1e-6
LRU_C = 8.0
CONV_W = 4
HEAD_DIM = 128
GATE_BLOCK = 128
LOG2E = math.log2(math.e)

LANES = 128
SUBLANES = 8
VMEM_LIMIT_BYTES = 56 * 1024 * 1024

MOD_TN = 1024
TOKEN_TS = 32
ATTN_T = 256
ATTN_HEADS_PER_STEP = 4
UP_TM, UP_TN = 2048, 512
UP_ROWS_PER_DOT = 1024
DOWN_TM, DOWN_TN = 1024, 512


def _params(n_axes):
    return pltpu.CompilerParams(
        dimension_semantics=("arbitrary",) * n_axes,
        vmem_limit_bytes=VMEM_LIMIT_BYTES,
    )


def _dot(a, b):
    return jnp.dot(a, b, preferred_element_type=F32)


def _rms_scale(x):
    return lax.rsqrt(jnp.mean(x * x, axis=-1, keepdims=True) + EPS)


def _mod_kernel(c_ref, w_ref, b_ref, o_ref):
    c = c_ref[...]
    ca = (c * jax.nn.sigmoid(c)).astype(BF16)
    o_ref[...] = _dot(ca, w_ref[...].astype(BF16)) + b_ref[...]


def _mod(c, w_ada, b_ada):
    bsz, d = c.shape
    n = w_ada.shape[1]
    return pl.pallas_call(
        _mod_kernel,
        grid=(n // MOD_TN,),
        in_specs=[
            pl.BlockSpec((bsz, d), lambda j: (0, 0)),
            pl.BlockSpec((d, MOD_TN), lambda j: (0, j)),
            pl.BlockSpec((1, MOD_TN), lambda j: (0, j)),
        ],
        out_specs=pl.BlockSpec((bsz, MOD_TN), lambda j: (0, j)),
        out_shape=jax.ShapeDtypeStruct((bsz, n), F32),
        compiler_params=_params(1),
        name="mod",
    )(c, w_ada, b_ada.reshape(1, n))


def _sigmoid(x):
    return 1.0 / (1.0 + jnp.exp2(x * -LOG2E))


def _gelu_tanh(y):
    c1 = -2.0 * math.sqrt(2.0 / math.pi) * LOG2E
    c3 = c1 * 0.044715
    return y / (1.0 + jnp.exp2(y * (c1 + c3 * (y * y))))


def _inproj_kernel(x_ref, g_ref, sc_ref, sh_ref, w_ref, wf_ref, gq_ref, gk_ref,
                   cw_ref, cb_ref, wax_ref, bax_ref, lam_ref,
                   yrec_ref, q_ref, k_ref, v_ref, fl_ref,
                   xs_ref, gate_ref, hs_ref, hstate_ref, *, d_rec, d_att, n_heads):
    bsz, ts, d = x_ref.shape
    rows = bsz * ts
    n_slabs = d_rec // LANES
    grp = SUBLANES
    n_groups = bsz // grp
    grp_rows = ts * grp
    pad = SUBLANES
    hist = CONV_W - 1
    step = pl.program_id(0)

    @pl.when(step == 0)
    def _():
        xs_ref[:, :, 0:pad * grp, :] = jnp.zeros((n_slabs, n_groups, pad * grp, LANES), F32)
        hstate_ref[...] = jnp.zeros(hstate_ref.shape, F32)

    @pl.when(step > 0)
    def _():
        xs_ref[:, :, (pad - hist) * grp:pad * grp, :] = xs_ref[:, :, (pad + ts - hist) * grp:(pad + ts) * grp, :]

    x = x_ref[...]
    y = x * _rms_scale(x) * g_ref[...]
    h = (y * (1.0 + sc_ref[...]) + sh_ref[...]).reshape(rows, d).astype(BF16)

    xr = _dot(h, w_ref[:, 0:d_rec])
    for cs in range(n_slabs):
        for b in range(bsz):
            dst = xs_ref.at[cs, b // grp]
            dst[pl.ds(pad * grp + b % grp, ts, stride=grp), :] = xr[b * ts:(b + 1) * ts, cs * LANES:(cs + 1) * LANES]
    gate_ref[...] = _gelu_tanh(_dot(h, w_ref[:, d_rec:2 * d_rec]))
    fl_ref[...] = _dot(h, wf_ref[...])[:, :n_heads].reshape(bsz, ts, n_heads)

    lam = lam_ref[...]
    softplus_neg = jnp.maximum(-lam, 0.0) + jnp.log1p(jnp.exp(-jnp.abs(lam)))
    neg_log_a_gain = LRU_C * softplus_neg
    log2_a_gain = neg_log_a_gain * -LOG2E
    base = (pad - hist) * grp

    def recurrent_slab(cs):
        ln = slice(cs * GATE_BLOCK, (cs + 1) * GATE_BLOCK)
        u_groups = []
        for g_idx in range(n_groups):
            ug = xs_ref[cs, g_idx, base:base + grp_rows, :] * cw_ref[0:1, ln]
            for kk in range(1, CONV_W):
                ug = ug + xs_ref[cs, g_idx, base + kk * grp:base + kk * grp + grp_rows, :] * cw_ref[kk:kk + 1, ln]
            u_groups.append(ug)
        u = jnp.concatenate(u_groups, axis=0) + cb_ref[:, ln]
        g = _dot(u.astype(BF16), wax_ref[cs]) + bax_ref[cs]
        r = _sigmoid(g[:, :GATE_BLOCK])
        i = _sigmoid(g[:, GATE_BLOCK:])
        a = jnp.exp2(r * log2_a_gain[:, ln])
        one_minus_a2 = jnp.tanh(r * neg_log_a_gain[:, ln]) * (1.0 + a * a)
        mult = jnp.where(one_minus_a2 > 0.0, one_minus_a2 * lax.rsqrt(one_minus_a2), 0.0)
        b_in = mult * (i * u)
        hcur = [hstate_ref[g_idx * grp:(g_idx + 1) * grp, ln] for g_idx in range(n_groups)]
        hseq = [[] for _ in range(n_groups)]
        for tt in range(ts):
            for g_idx in range(n_groups):
                rs = slice(g_idx * grp_rows + tt * grp, g_idx * grp_rows + (tt + 1) * grp)
                hcur[g_idx] = a[rs] * hcur[g_idx] + b_in[rs]
                hseq[g_idx].append(hcur[g_idx])
        for g_idx in range(n_groups):
            hstate_ref[g_idx * grp:(g_idx + 1) * grp, ln] = hcur[g_idx]
            hs_ref[cs, g_idx] = jnp.concatenate(hseq[g_idx], axis=0)
        for b in range(bsz):
            hb = hs_ref.at[cs, b // grp][pl.ds(b % grp, ts, stride=grp), :]
            yrec_ref[b, :, ln] = (hb * gate_ref[b * ts:(b + 1) * ts, ln]).astype(BF16)

    def project_heads(dst_ref, col0, gain):
        z = _dot(h, w_ref[:, col0:col0 + d_att])
        for hd in range(n_heads):
            zh = z[:, hd * HEAD_DIM:(hd + 1) * HEAD_DIM]
            if gain is not None:
                zh = zh * _rms_scale(zh) * gain
            dst_ref[:, hd] = zh.astype(BF16).reshape(bsz, ts, HEAD_DIM)

    off = 2 * d_rec
    project_heads(q_ref, off, gq_ref[...] * (HEAD_DIM ** -0.5 * LOG2E))
    project_heads(k_ref, off + d_att, gk_ref[...])
    project_heads(v_ref, off + 2 * d_att, None)
    for cs in range(n_slabs):
        recurrent_slab(cs)


def _inproj(x, g_mix, sc1, sh1, w_main, w_f, g_q, g_k, conv_w, conv_b, wax, bax, lam,
            *, d_rec, d_att, n_heads):
    bsz, seq, d = x.shape
    ts = TOKEN_TS
    n_slabs = d_rec // LANES
    const = lambda i: (0, 0)
    const3 = lambda i: (0, 0, 0)
    tchunk = lambda i: (0, i, 0)
    head_shape = jax.ShapeDtypeStruct((bsz, n_heads, seq, HEAD_DIM), BF16)
    head_spec = pl.BlockSpec((bsz, n_heads, ts, HEAD_DIM), lambda i: (0, 0, i, 0))
    return pl.pallas_call(
        functools.partial(_inproj_kernel, d_rec=d_rec, d_att=d_att, n_heads=n_heads),
        grid=(seq // ts,),
        in_specs=[
            pl.BlockSpec((bsz, ts, d), tchunk),
            pl.BlockSpec((1, d), const),
            pl.BlockSpec((bsz, 1, d), const3),
            pl.BlockSpec((bsz, 1, d), const3),
            pl.BlockSpec((d, 2 * d_rec + 3 * d_att), const, pipeline_mode=pl.Buffered(1)),
            pl.BlockSpec(w_f.shape, const, pipeline_mode=pl.Buffered(1)),
            pl.BlockSpec((1, HEAD_DIM), const),
            pl.BlockSpec((1, HEAD_DIM), const),
            pl.BlockSpec(conv_w.shape, const),
            pl.BlockSpec(conv_b.shape, const),
            pl.BlockSpec(wax.shape, const3),
            pl.BlockSpec(bax.shape, const3),
            pl.BlockSpec(lam.shape, const),
        ],
        out_specs=[
            pl.BlockSpec((bsz, ts, d_rec), tchunk),
            head_spec, head_spec, head_spec,
            pl.BlockSpec((bsz, ts, n_heads), tchunk),
        ],
        out_shape=[
            jax.ShapeDtypeStruct((bsz, seq, d_rec), BF16),
            head_shape, head_shape, head_shape,
            jax.ShapeDtypeStruct((bsz, seq, n_heads), F32),
        ],
        scratch_shapes=[
            pltpu.VMEM((n_slabs, bsz // SUBLANES, (SUBLANES + ts) * SUBLANES, LANES), F32),
            pltpu.VMEM((bsz * ts, d_rec), F32),
            pltpu.VMEM((n_slabs, bsz // SUBLANES, ts * SUBLANES, LANES), F32),
            pltpu.VMEM((bsz, d_rec), F32),
        ],
        compiler_params=_params(1),
        name="inproj",
    )(x, g_mix, sc1, sh1, w_main, w_f, g_q, g_k, conv_w, conv_b, wax, bax, lam)


def _cum_kernel(f_ref, b_ref, o_ref):
    z = f_ref[...] + b_ref[...]
    x = -(jnp.maximum(-z, 0.0) + jnp.log1p(jnp.exp(-jnp.abs(z))))
    n = x.shape[-1]
    lane = lax.broadcasted_iota(jnp.int32, x.shape, 1)
    d = 1
    while d < n:
        x = x + jnp.where(lane >= d, pltpu.roll(x, d, axis=1), 0.0)
        d *= 2
    o_ref[...] = x


def _cum(f_rows, b_rows):
    r, n = f_rows.shape
    return pl.pallas_call(
        _cum_kernel,
        grid=(1,),
        in_specs=[pl.BlockSpec((r, n), lambda i: (0, 0)),
                  pl.BlockSpec((r, 1), lambda i: (0, 0))],
        out_specs=pl.BlockSpec((r, n), lambda i: (0, 0)),
        out_shape=jax.ShapeDtypeStruct((r, n), F32),
        compiler_params=_params(1),
        name="cum",
    )(f_rows, b_rows)


def _attn_kernel(q_ref, k_ref, v_ref, cum_ref, o_ref):
    seq = q_ref.shape[2]
    t = ATTN_T
    n_blocks = seq // t
    row = lax.broadcasted_iota(jnp.int32, (t, t), 0)
    col = lax.broadcasted_iota(jnp.int32, (t, t), 1)
    causal = col <= row
    ones_cols = jnp.ones((t, HEAD_DIM), BF16)

    n_h = q_ref.shape[1]
    m = [None] * n_h
    acc = [None] * n_h
    for kj in range(n_blocks):
        blk = slice(kj * t, (kj + 1) * t)
        last = kj == n_blocks - 1
        for hh in range(n_h):
            q = q_ref[0, hh, kj * t:, :]
            kb = k_ref[0, hh, blk, :]
            vb = jnp.concatenate([v_ref[0, hh, blk, :], ones_cols], axis=1)
            s = lax.dot_general(q, kb, (((1,), (1,)), ((), ())),
                                preferred_element_type=F32) - cum_ref[0, hh, kj] * LOG2E
            s_diag = jnp.where(causal, s[:t], -jnp.inf)
            s = s_diag if last else jnp.concatenate([s_diag, s[t:]], axis=0)
            m_blk = jnp.max(s, axis=1, keepdims=True)
            if kj == 0:
                m_new = m_blk
                acc_new = _dot(jnp.exp2(s - m_new).astype(BF16), vb)
            else:
                m_new = jnp.maximum(m[hh], m_blk)
                acc_new = jnp.exp2(m[hh] - m_new) * acc[hh] + _dot(jnp.exp2(s - m_new).astype(BF16), vb)
            o_ref[0, hh, blk, :] = (acc_new[:t, :HEAD_DIM] / acc_new[:t, HEAD_DIM:]).astype(BF16)
            if not last:
                m[hh], acc[hh] = m_new[t:], acc_new[t:]


def _attn(q, k, v, cum):
    bsz, n_heads, seq, dh = q.shape
    cum5 = cum.reshape(bsz, n_heads, seq // ATTN_T, 1, ATTN_T)
    hps = ATTN_HEADS_PER_STEP
    spec = pl.BlockSpec((1, hps, seq, dh), lambda b, h: (b, h, 0, 0))
    return pl.pallas_call(
        _attn_kernel,
        grid=(bsz, n_heads // hps),
        in_specs=[spec, spec, spec,
                  pl.BlockSpec((1, hps) + cum5.shape[2:], lambda b, h: (b, h, 0, 0, 0))],
        out_specs=spec,
        out_shape=jax.ShapeDtypeStruct(q.shape, BF16),
        compiler_params=_params(2),
        name="attn",
    )(q, k, v, cum5)


def _outproj_kernel(yrec_ref, yatt_ref, x_ref, w_ref, gor_ref, goa_ref, gt_ref,
                    gf_ref, sc_ref, sh_ref, x1_ref, h2_ref, *, n_heads):
    bsz, ts, d = x_ref.shape
    yr = yrec_ref[...].astype(F32).reshape(bsz * ts, yrec_ref.shape[2])
    parts = [(yr * _rms_scale(yr) * gor_ref[...]).astype(BF16)]
    heads = [yatt_ref[:, hd].astype(F32).reshape(bsz * ts, HEAD_DIM) for hd in range(n_heads)]
    ssq = heads[0] * heads[0]
    for hh in heads[1:]:
        ssq = ssq + hh * hh
    inv = lax.rsqrt(jnp.sum(ssq, axis=-1, keepdims=True) / (n_heads * HEAD_DIM) + EPS)
    for hd in range(n_heads):
        gain = goa_ref[:, hd * HEAD_DIM:(hd + 1) * HEAD_DIM]
        parts.append((heads[hd] * inv * gain).astype(BF16))
    mix = jnp.concatenate(parts, axis=1)
    x1 = x_ref[...] + gt_ref[...] * _dot(mix, w_ref[...]).reshape(bsz, ts, d)
    x1_ref[...] = x1
    y = x1 * _rms_scale(x1) * gf_ref[...]
    h2_ref[...] = (y * (1.0 + sc_ref[...]) + sh_ref[...]).astype(BF16)


def _outproj(y_rec, y_att, x, w_out, g_or, g_oa, gt1, g_ffn, sc2, sh2):
    bsz, seq, d = x.shape
    _, n_heads, _, dh = y_att.shape
    d_rec = y_rec.shape[2]
    ts = TOKEN_TS
    const = lambda i: (0, 0)
    const3 = lambda i: (0, 0, 0)
    tchunk = lambda i: (0, i, 0)
    tok_spec = pl.BlockSpec((bsz, ts, d), tchunk)
    mod_spec = pl.BlockSpec((bsz, 1, d), const3)
    return pl.pallas_call(
        functools.partial(_outproj_kernel, n_heads=n_heads),
        grid=(seq // ts,),
        in_specs=[
            pl.BlockSpec((bsz, ts, d_rec), tchunk),
            pl.BlockSpec((bsz, n_heads, ts, dh), lambda i: (0, 0, i, 0)),
            tok_spec,
            pl.BlockSpec(w_out.shape, const, pipeline_mode=pl.Buffered(1)),
            pl.BlockSpec((1, d_rec), const),
            pl.BlockSpec((1, n_heads * dh), const),
            mod_spec,
            pl.BlockSpec((1, d), const),
            mod_spec,
            mod_spec,
        ],
        out_specs=[tok_spec, tok_spec],
        out_shape=[jax.ShapeDtypeStruct(x.shape, F32), jax.ShapeDtypeStruct(x.shape, BF16)],
        compiler_params=_params(1),
        name="outproj",
    )(y_rec, y_att, x, w_out, g_or, g_oa, gt1, g_ffn, sc2, sh2)


def _up_kernel(h_ref, wg_ref, wu_ref, a_ref, wg_b, wu_b):
    @pl.when(pl.program_id(1) == 0)
    def _():
        wg_b[...] = wg_ref[...].astype(BF16)
        wu_b[...] = wu_ref[...].astype(BF16)

    for r0 in range(0, h_ref.shape[0], UP_ROWS_PER_DOT):
        rows = slice(r0, r0 + UP_ROWS_PER_DOT)
        h = h_ref[rows, :]
        g = _dot(h, wg_b[...])
        u = _dot(h, wu_b[...])
        a_ref[rows, :] = (g * jax.nn.sigmoid(g) * u).astype(BF16)


def _up(h2, w_up, d_ff):
    t, d = h2.shape
    tm, tn = UP_TM, UP_TN
    nj = d_ff // tn
    return pl.pallas_call(
        _up_kernel,
        grid=(nj, t // tm),
        in_specs=[
            pl.BlockSpec((tm, d), lambda j, i: (i, 0)),
            pl.BlockSpec((d, tn), lambda j, i: (0, j)),
            pl.BlockSpec((d, tn), lambda j, i: (0, j + nj)),
        ],
        out_specs=pl.BlockSpec((tm, tn), lambda j, i: (i, j)),
        out_shape=jax.ShapeDtypeStruct((t, d_ff), BF16),
        scratch_shapes=[pltpu.VMEM((d, tn), BF16), pltpu.VMEM((d, tn), BF16)],
        compiler_params=_params(2),
        name="up",
    )(h2, w_up, w_up)


def _down_kernel(a_ref, w_ref, x1_ref, gt_ref, o_ref):
    o_ref[...] = x1_ref[...] + gt_ref[0] * _dot(a_ref[...], w_ref[...])


def _down(a, w_down, x1, gt2, *, seq):
    t, d_ff = a.shape
    d = w_down.shape[1]
    tm, tn = DOWN_TM, DOWN_TN
    spb = seq // tm
    return pl.pallas_call(
        _down_kernel,
        grid=(t // tm, d // tn),
        in_specs=[
            pl.BlockSpec((tm, d_ff), lambda i, j: (i, 0)),
            pl.BlockSpec((d_ff, tn), lambda i, j: (0, j)),
            pl.BlockSpec((tm, tn), lambda i, j: (i, j)),
            pl.BlockSpec((1, 1, tn), lambda i, j: (i // spb, 0, j)),
        ],
        out_specs=pl.BlockSpec((tm, tn), lambda i, j: (i, j)),
        out_shape=jax.ShapeDtypeStruct((t, d), F32),
        compiler_params=_params(2),
        name="down",
    )(a, w_down, x1, gt2)


def kernel(x, c, w_ada, b_ada, g_mix, w_in, conv_w, conv_b, w_gate_a, b_gate_a, w_gate_x, b_gate_x,
           lru_logit, b_forget, g_q, g_k, g_out_rec, g_out_att, w_out, g_ffn, w_up, w_down):
    bsz, seq, d = x.shape
    depth = w_ada.shape[0]
    d_rec = conv_w.shape[-1]
    n_heads = b_forget.shape[-1]
    d_att = n_heads * HEAD_DIM
    n_gate_blocks = w_gate_a.shape[1]
    d_ff = w_down.shape[1]
    t = bsz * seq
    row = lambda v: v.reshape(1, -1)

    for l in range(depth):
        mod = _mod(c, w_ada[l], b_ada[l])
        sh1, sc1, gt1, sh2, sc2, gt2 = [m.reshape(bsz, 1, d) for m in jnp.split(mod, 6, axis=-1)]

        w_in_b = w_in[l].astype(BF16)
        w_f = jnp.pad(w_in_b[:, 2 * d_rec + 3 * d_att:], ((0, 0), (0, LANES - n_heads)))
        wax = jnp.concatenate([w_gate_a[l], w_gate_x[l]], axis=-1).astype(BF16)
        bax = jnp.concatenate([b_gate_a[l].reshape(n_gate_blocks, 1, GATE_BLOCK),
                               b_gate_x[l].reshape(n_gate_blocks, 1, GATE_BLOCK)], axis=-1)
        y_rec, q, k, v, fl = _inproj(x, row(g_mix[l]), sc1, sh1, w_in_b, w_f, row(g_q[l]), row(g_k[l]),
                                     conv_w[l], row(conv_b[l]), wax, bax, row(lru_logit[l]),
                                     d_rec=d_rec, d_att=d_att, n_heads=n_heads)

        f_rows = fl.transpose(0, 2, 1).reshape(bsz * n_heads, seq)
        b_rows = jnp.tile(b_forget[l], bsz).reshape(bsz * n_heads, 1)
        cum = _cum(f_rows, b_rows)
        y_att = _attn(q, k, v, cum)

        x1, h2 = _outproj(y_rec, y_att, x, w_out[l].astype(BF16), row(g_out_rec[l]), row(g_out_att[l]),
                          gt1, row(g_ffn[l]), sc2, sh2)
        a = _up(h2.reshape(t, d), w_up[l], d_ff)
        x = _down(a, w_down[l].astype(BF16), x1.reshape(t, d), gt2, seq=seq).reshape(bsz, seq, d)
    return x
```

```python
import functools
import math

import jax
import jax.numpy as jnp
from jax import lax
from jax.experimental import pallas as pl
from jax.experimental.pallas import tpu as pltpu

F32 = jnp.float32
BF16 = jnp.bfloat16

EPS = 1e-6
LRU_C = 8.0
CONV_W = 4
HEAD_DIM = 128
GATE_BLOCK = 128
LOG2E = math.log2(math.e)

LANES = 128
SUBLANES = 8
VMEM_LIMIT_BYTES = 56 * 1024 * 1024

MOD_TN = 1024
TOKEN_TS = 32
ATTN_T = 256
ATTN_HEADS_PER_STEP = 4
UP_TM, UP_TN = 2048, 512
UP_ROWS_PER_DOT = 1024
DOWN_TM, DOWN_TN = 1024, 512


def _params(n_axes):
    return pltpu.CompilerParams(
        dimension_semantics=("arbitrary",) * n_axes,
        vmem_limit_bytes=VMEM_LIMIT_BYTES,
    )


def _dot(a, b):
    return jnp.dot(a, b, preferred_element_type=F32)


def _rms_scale(x):
    return lax.rsqrt(jnp.mean(x * x, axis=-1, keepdims=True) + EPS)


def _mod_kernel(c_ref, w_ref, b_ref, o_ref):
    c = c_ref[...]
    ca = (c * jax.nn.sigmoid(c)).astype(BF16)
    o_ref[...] = _dot(ca, w_ref[...].astype(BF16)) + b_ref[...]


def _mod(c, w_ada, b_ada):
    bsz, d = c.shape
    n = w_ada.shape[1]
    return pl.pallas_call(
        _mod_kernel,
        grid=(n // MOD_TN,),
        in_specs=[
            pl.BlockSpec((bsz, d), lambda j: (0, 0)),
            pl.BlockSpec((d, MOD_TN), lambda j: (0, j)),
            pl.BlockSpec((1, MOD_TN), lambda j: (0, j)),
        ],
        out_specs=pl.BlockSpec((bsz, MOD_TN), lambda j: (0, j)),
        out_shape=jax.ShapeDtypeStruct((bsz, n), F32),
        compiler_params=_params(1),
        name="mod",
    )(c, w_ada, b_ada.reshape(1, n))


def _sigmoid(x):
    return 1.0 / (1.0 + jnp.exp2(x * -LOG2E))


def _gelu_tanh(y):
    c1 = -2.0 * math.sqrt(2.0 / math.pi) * LOG2E
    c3 = c1 * 0.044715
    return y / (1.0 + jnp.exp2(y * (c1 + c3 * (y * y))))


def _inproj_kernel(x_ref, g_ref, sc_ref, sh_ref, w_ref, wf_ref, gq_ref, gk_ref,
                   cw_ref, cb_ref, wax_ref, bax_ref, lam_ref,
                   yrec_ref, q_ref, k_ref, v_ref, fl_ref,
                   xs_ref, gate_ref, hs_ref, hstate_ref, *, d_rec, d_att, n_heads):
    bsz, ts, d = x_ref.shape
    rows = bsz * ts
    n_slabs = d_rec // LANES
    grp = SUBLANES
    n_groups = bsz // grp
    grp_rows = ts * grp
    pad = SUBLANES
    hist = CONV_W - 1
    step = pl.program_id(0)

    @pl.when(step == 0)
    def _():
        xs_ref[:, :, 0:pad * grp, :] = jnp.zeros((n_slabs, n_groups, pad * grp, LANES), F32)
        hstate_ref[...] = jnp.zeros(hstate_ref.shape, F32)

    @pl.when(step > 0)
    def _():
        xs_ref[:, :, (pad - hist) * grp:pad * grp, :] = xs_ref[:, :, (pad + ts - hist) * grp:(pad + ts) * grp, :]

    x = x_ref[...]
    y = x * _rms_scale(x) * g_ref[...]
    h = (y * (1.0 + sc_ref[...]) + sh_ref[...]).reshape(rows, d).astype(BF16)

    xr = _dot(h, w_ref[:, 0:d_rec])
    for cs in range(n_slabs):
        for b in range(bsz):
            dst = xs_ref.at[cs, b // grp]
            dst[pl.ds(pad * grp + b % grp, ts, stride=grp), :] = xr[b * ts:(b + 1) * ts, cs * LANES:(cs + 1) * LANES]
    gate_ref[...] = _gelu_tanh(_dot(h, w_ref[:, d_rec:2 * d_rec]))
    fl_ref[...] = _dot(h, wf_ref[...])[:, :n_heads].reshape(bsz, ts, n_heads)

    lam = lam_ref[...]
    softplus_neg = jnp.maximum(-lam, 0.0) + jnp.log1p(jnp.exp(-jnp.abs(lam)))
    neg_log_a_gain = LRU_C * softplus_neg
    log2_a_gain = neg_log_a_gain * -LOG2E
    base = (pad - hist) * grp

    def recurrent_slab(cs):
        ln = slice(cs * GATE_BLOCK, (cs + 1) * GATE_BLOCK)
        u_groups = []
        for g_idx in range(n_groups):
            ug = xs_ref[cs, g_idx, base:base + grp_rows, :] * cw_ref[0:1, ln]
            for kk in range(1, CONV_W):
                ug = ug + xs_ref[cs, g_idx, base + kk * grp:base + kk * grp + grp_rows, :] * cw_ref[kk:kk + 1, ln]
            u_groups.append(ug)
        u = jnp.concatenate(u_groups, axis=0) + cb_ref[:, ln]
        g = _dot(u.astype(BF16), wax_ref[cs]) + bax_ref[cs]
        r = _sigmoid(g[:, :GATE_BLOCK])
        i = _sigmoid(g[:, GATE_BLOCK:])
        a = jnp.exp2(r * log2_a_gain[:, ln])
        one_minus_a2 = jnp.tanh(r * neg_log_a_gain[:, ln]) * (1.0 + a * a)
        mult = jnp.where(one_minus_a2 > 0.0, one_minus_a2 * lax.rsqrt(one_minus_a2), 0.0)
        b_in = mult * (i * u)
        hcur = [hstate_ref[g_idx * grp:(g_idx + 1) * grp, ln] for g_idx in range(n_groups)]
        hseq = [[] for _ in range(n_groups)]
        for tt in range(ts):
            for g_idx in range(n_groups):
                rs = slice(g_idx * grp_rows + tt * grp, g_idx * grp_rows + (tt + 1) * grp)
                hcur[g_idx] = a[rs] * hcur[g_idx] + b_in[rs]
                hseq[g_idx].append(hcur[g_idx])
        for g_idx in range(n_groups):
            hstate_ref[g_idx * grp:(g_idx + 1) * grp, ln] = hcur[g_idx]
            hs_ref[cs, g_idx] = jnp.concatenate(hseq[g_idx], axis=0)
        for b in range(bsz):
            hb = hs_ref.at[cs, b // grp][pl.ds(b % grp, ts, stride=grp), :]
            yrec_ref[b, :, ln] = (hb * gate_ref[b * ts:(b + 1) * ts, ln]).astype(BF16)

    def project_heads(dst_ref, col0, gain):
        z = _dot(h, w_ref[:, col0:col0 + d_att])
        for hd in range(n_heads):
            zh = z[:, hd * HEAD_DIM:(hd + 1) * HEAD_DIM]
            if gain is not None:
                zh = zh * _rms_scale(zh) * gain
            dst_ref[:, hd] = zh.astype(BF16).reshape(bsz, ts, HEAD_DIM)

    off = 2 * d_rec
    project_heads(q_ref, off, gq_ref[...] * (HEAD_DIM ** -0.5 * LOG2E))
    project_heads(k_ref, off + d_att, gk_ref[...])
    project_heads(v_ref, off + 2 * d_att, None)
    for cs in range(n_slabs):
        recurrent_slab(cs)


def _inproj(x, g_mix, sc1, sh1, w_main, w_f, g_q, g_k, conv_w, conv_b, wax, bax, lam,
            *, d_rec, d_att, n_heads):
    bsz, seq, d = x.shape
    ts = TOKEN_TS
    n_slabs = d_rec // LANES
    const = lambda i: (0, 0)
    const3 = lambda i: (0, 0, 0)
    tchunk = lambda i: (0, i, 0)
    head_shape = jax.ShapeDtypeStruct((bsz, n_heads, seq, HEAD_DIM), BF16)
    head_spec = pl.BlockSpec((bsz, n_heads, ts, HEAD_DIM), lambda i: (0, 0, i, 0))
    return pl.pallas_call(
        functools.partial(_inproj_kernel, d_rec=d_rec, d_att=d_att, n_heads=n_heads),
        grid=(seq // ts,),
        in_specs=[
            pl.BlockSpec((bsz, ts, d), tchunk),
            pl.BlockSpec((1, d), const),
            pl.BlockSpec((bsz, 1, d), const3),
            pl.BlockSpec((bsz, 1, d), const3),
            pl.BlockSpec((d, 2 * d_rec + 3 * d_att), const, pipeline_mode=pl.Buffered(1)),
            pl.BlockSpec(w_f.shape, const, pipeline_mode=pl.Buffered(1)),
            pl.BlockSpec((1, HEAD_DIM), const),
            pl.BlockSpec((1, HEAD_DIM), const),
            pl.BlockSpec(conv_w.shape, const),
            pl.BlockSpec(conv_b.shape, const),
            pl.BlockSpec(wax.shape, const3),
            pl.BlockSpec(bax.shape, const3),
            pl.BlockSpec(lam.shape, const),
        ],
        out_specs=[
            pl.BlockSpec((bsz, ts, d_rec), tchunk),
            head_spec, head_spec, head_spec,
            pl.BlockSpec((bsz, ts, n_heads), tchunk),
        ],
        out_shape=[
            jax.ShapeDtypeStruct((bsz, seq, d_rec), BF16),
            head_shape, head_shape, head_shape,
            jax.ShapeDtypeStruct((bsz, seq, n_heads), F32),
        ],
        scratch_shapes=[
            pltpu.VMEM((n_slabs, bsz // SUBLANES, (SUBLANES + ts) * SUBLANES, LANES), F32),
            pltpu.VMEM((bsz * ts, d_rec), F32),
            pltpu.VMEM((n_slabs, bsz // SUBLANES, ts * SUBLANES, LANES), F32),
            pltpu.VMEM((bsz, d_rec), F32),
        ],
        compiler_params=_params(1),
        name="inproj",
    )(x, g_mix, sc1, sh1, w_main, w_f, g_q, g_k, conv_w, conv_b, wax, bax, lam)


def _cum_kernel(f_ref, b_ref, o_ref):
    z = f_ref[...] + b_ref[...]
    x = -(jnp.maximum(-z, 0.0) + jnp.log1p(jnp.exp(-jnp.abs(z))))
    n = x.shape[-1]
    lane = lax.broadcasted_iota(jnp.int32, x.shape, 1)
    d = 1
    while d < n:
        x = x + jnp.where(lane >= d, pltpu.roll(x, d, axis=1), 0.0)
        d *= 2
    o_ref[...] = x


def _cum(f_rows, b_rows):
    r, n = f_rows.shape
    return pl.pallas_call(
        _cum_kernel,
        grid=(1,),
        in_specs=[pl.BlockSpec((r, n), lambda i: (0, 0)),
                  pl.BlockSpec((r, 1), lambda i: (0, 0))],
        out_specs=pl.BlockSpec((r, n), lambda i: (0, 0)),
        out_shape=jax.ShapeDtypeStruct((r, n), F32),
        compiler_params=_params(1),
        name="cum",
    )(f_rows, b_rows)


def _attn_kernel(q_ref, k_ref, v_ref, cum_ref, o_ref):
    seq = q_ref.shape[2]
    t = ATTN_T
    n_blocks = seq // t
    row = lax.broadcasted_iota(jnp.int32, (t, t), 0)
    col = lax.broadcasted_iota(jnp.int32, (t, t), 1)
    causal = col <= row
    ones_cols = jnp.ones((t, HEAD_DIM), BF16)

    n_h = q_ref.shape[1]
    m = [None] * n_h
    acc = [None] * n_h
    for kj in range(n_blocks):
        blk = slice(kj * t, (kj + 1) * t)
        last = kj == n_blocks - 1
        for hh in range(n_h):
            q = q_ref[0, hh, kj * t:, :]
            kb = k_ref[0, hh, blk, :]
            vb = jnp.concatenate([v_ref[0, hh, blk, :], ones_cols], axis=1)
            s = lax.dot_general(q, kb, (((1,), (1,)), ((), ())),
                                preferred_element_type=F32) - cum_ref[0, hh, kj] * LOG2E
            s_diag = jnp.where(causal, s[:t], -jnp.inf)
            s = s_diag if last else jnp.concatenate([s_diag, s[t:]], axis=0)
            m_blk = jnp.max(s, axis=1, keepdims=True)
            if kj == 0:
                m_new = m_blk
                acc_new = _dot(jnp.exp2(s - m_new).astype(BF16), vb)
            else:
                m_new = jnp.maximum(m[hh], m_blk)
                acc_new = jnp.exp2(m[hh] - m_new) * acc[hh] + _dot(jnp.exp2(s - m_new).astype(BF16), vb)
            o_ref[0, hh, blk, :] = (acc_new[:t, :HEAD_DIM] / acc_new[:t, HEAD_DIM:]).astype(BF16)
            if not last:
                m[hh], acc[hh] = m_new[t:], acc_new[t:]


def _attn(q, k, v, cum):
    bsz, n_heads, seq, dh = q.shape
    cum5 = cum.reshape(bsz, n_heads, seq // ATTN_T, 1, ATTN_T)
    hps = ATTN_HEADS_PER_STEP
    spec = pl.BlockSpec((1, hps, seq, dh), lambda b, h: (b, h, 0, 0))
    return pl.pallas_call(
        _attn_kernel,
        grid=(bsz, n_heads // hps),
        in_specs=[spec, spec, spec,
                  pl.BlockSpec((1, hps) + cum5.shape[2:], lambda b, h: (b, h, 0, 0, 0))],
        out_specs=spec,
        out_shape=jax.ShapeDtypeStruct(q.shape, BF16),
        compiler_params=_params(2),
        name="attn",
    )(q, k, v, cum5)


def _outproj_kernel(yrec_ref, yatt_ref, x_ref, w_ref, gor_ref, goa_ref, gt_ref,
                    gf_ref, sc_ref, sh_ref, x1_ref, h2_ref, *, n_heads):
    bsz, ts, d = x_ref.shape
    yr = yrec_ref[...].astype(F32).reshape(bsz * ts, yrec_ref.shape[2])
    parts = [(yr * _rms_scale(yr) * gor_ref[...]).astype(BF16)]
    heads = [yatt_ref[:, hd].astype(F32).reshape(bsz * ts, HEAD_DIM) for hd in range(n_heads)]
    ssq = heads[0] * heads[0]
    for hh in heads[1:]:
        ssq = ssq + hh * hh
    inv = lax.rsqrt(jnp.sum(ssq, axis=-1, keepdims=True) / (n_heads * HEAD_DIM) + EPS)
    for hd in range(n_heads):
        gain = goa_ref[:, hd * HEAD_DIM:(hd + 1) * HEAD_DIM]
        parts.append((heads[hd] * inv * gain).astype(BF16))
    mix = jnp.concatenate(parts, axis=1)
    x1 = x_ref[...] + gt_ref[...] * _dot(mix, w_ref[...]).reshape(bsz, ts, d)
    x1_ref[...] = x1
    y = x1 * _rms_scale(x1) * gf_ref[...]
    h2_ref[...] = (y * (1.0 + sc_ref[...]) + sh_ref[...]).astype(BF16)


def _outproj(y_rec, y_att, x, w_out, g_or, g_oa, gt1, g_ffn, sc2, sh2):
    bsz, seq, d = x.shape
    _, n_heads, _, dh = y_att.shape
    d_rec = y_rec.shape[2]
    ts = TOKEN_TS
    const = lambda i: (0, 0)
    const3 = lambda i: (0, 0, 0)
    tchunk = lambda i: (0, i, 0)
    tok_spec = pl.BlockSpec((bsz, ts, d), tchunk)
    mod_spec = pl.BlockSpec((bsz, 1, d), const3)
    return pl.pallas_call(
        functools.partial(_outproj_kernel, n_heads=n_heads),
        grid=(seq // ts,),
        in_specs=[
            pl.BlockSpec((bsz, ts, d_rec), tchunk),
            pl.BlockSpec((bsz, n_heads, ts, dh), lambda i: (0, 0, i, 0)),
            tok_spec,
            pl.BlockSpec(w_out.shape, const, pipeline_mode=pl.Buffered(1)),
            pl.BlockSpec((1, d_rec), const),
            pl.BlockSpec((1, n_heads * dh), const),
            mod_spec,
            pl.BlockSpec((1, d), const),
            mod_spec,
            mod_spec,
        ],
        out_specs=[tok_spec, tok_spec],
        out_shape=[jax.ShapeDtypeStruct(x.shape, F32), jax.ShapeDtypeStruct(x.shape, BF16)],
        compiler_params=_params(1),
        name="outproj",
    )(y_rec, y_att, x, w_out, g_or, g_oa, gt1, g_ffn, sc2, sh2)


def _up_kernel(h_ref, wg_ref, wu_ref, a_ref, wg_b, wu_b):
    @pl.when(pl.program_id(1) == 0)
    def _():
        wg_b[...] = wg_ref[...].astype(BF16)
        wu_b[...] = wu_ref[...].astype(BF16)

    for r0 in range(0, h_ref.shape[0], UP_ROWS_PER_DOT):
        rows = slice(r0, r0 + UP_ROWS_PER_DOT)
        h = h_ref[rows, :]
        g = _dot(h, wg_b[...])
        u = _dot(h, wu_b[...])
        a_ref[rows, :] = (g * jax.nn.sigmoid(g) * u).astype(BF16)


def _up(h2, w_up, d_ff):
    t, d = h2.shape
    tm, tn = UP_TM, UP_TN
    nj = d_ff // tn
    return pl.pallas_call(
        _up_kernel,
        grid=(nj, t // tm),
        in_specs=[
            pl.BlockSpec((tm, d), lambda j, i: (i, 0)),
            pl.BlockSpec((d, tn), lambda j, i: (0, j)),
            pl.BlockSpec((d, tn), lambda j, i: (0, j + nj)),
        ],
        out_specs=pl.BlockSpec((tm, tn), lambda j, i: (i, j)),
        out_shape=jax.ShapeDtypeStruct((t, d_ff), BF16),
        scratch_shapes=[pltpu.VMEM((d, tn), BF16), pltpu.VMEM((d, tn), BF16)],
        compiler_params=_params(2),
        name="up",
    )(h2, w_up, w_up)


def _down_kernel(a_hbm, w_ref, x1_ref, gt_ref, o_ref, a_buf, sems):
    i, j = pl.program_id(0), pl.program_id(1)
    n_i, n_j = pl.num_programs(0), pl.num_programs(1)
    tm = a_buf.shape[1]
    rows = tm // sems.shape[1]
    slot = i % 2

    def slice_copy(tile, part, dst_slot):
        r0 = pl.multiple_of(part * rows, rows)
        return pltpu.make_async_copy(a_hbm.at[pl.ds(pl.multiple_of(tile * tm, tm) + r0, rows), :],
                                     a_buf.at[dst_slot, pl.ds(r0, rows), :],
                                     sems.at[dst_slot, part])

    @pl.when((i == 0) & (j == 0))
    def _():
        for part in range(sems.shape[1]):
            slice_copy(0, part, 0).start()

    @pl.when(j == 0)
    def _():
        for part in range(sems.shape[1]):
            slice_copy(i, part, slot).wait()

    @pl.when(i + 1 < n_i)
    def _():
        slice_copy(i + 1, j, 1 - slot).start(priority=1)

    o_ref[...] = x1_ref[...] + gt_ref[0] * _dot(a_buf[slot], w_ref[...])


def _down(a, w_down, x1, gt2, *, seq):
    t, d_ff = a.shape
    d = w_down.shape[1]
    tm, tn = DOWN_TM, DOWN_TN
    spb = seq // tm
    n_j = d // tn
    assert tm % (n_j * SUBLANES) == 0
    return pl.pallas_call(
        _down_kernel,
        grid=(t // tm, n_j),
        in_specs=[
            pl.BlockSpec(memory_space=pl.ANY),
            pl.BlockSpec((d_ff, tn), lambda i, j: (0, j)),
            pl.BlockSpec((tm, tn), lambda i, j: (i, j)),
            pl.BlockSpec((1, 1, tn), lambda i, j: (i // spb, 0, j)),
        ],
        out_specs=pl.BlockSpec((tm, tn), lambda i, j: (i, j)),
        out_shape=jax.ShapeDtypeStruct((t, d), F32),
        scratch_shapes=[pltpu.VMEM((2, tm, d_ff), BF16), pltpu.SemaphoreType.DMA((2, n_j))],
        compiler_params=_params(2),
        name="down",
    )(a, w_down, x1, gt2)


def kernel(x, c, w_ada, b_ada, g_mix, w_in, conv_w, conv_b, w_gate_a, b_gate_a, w_gate_x, b_gate_x,
           lru_logit, b_forget, g_q, g_k, g_out_rec, g_out_att, w_out, g_ffn, w_up, w_down):
    bsz, seq, d = x.shape
    depth = w_ada.shape[0]
    d_rec = conv_w.shape[-1]
    n_heads = b_forget.shape[-1]
    d_att = n_heads * HEAD_DIM
    n_gate_blocks = w_gate_a.shape[1]
    d_ff = w_down.shape[1]
    t = bsz * seq
    row = lambda v: v.reshape(1, -1)

    for l in range(depth):
        mod = _mod(c, w_ada[l], b_ada[l])
        sh1, sc1, gt1, sh2, sc2, gt2 = [m.reshape(bsz, 1, d) for m in jnp.split(mod, 6, axis=-1)]

        w_in_b = w_in[l].astype(BF16)
        w_f = jnp.pad(w_in_b[:, 2 * d_rec + 3 * d_att:], ((0, 0), (0, LANES - n_heads)))
        wax = jnp.concatenate([w_gate_a[l], w_gate_x[l]], axis=-1).astype(BF16)
        bax = jnp.concatenate([b_gate_a[l].reshape(n_gate_blocks, 1, GATE_BLOCK),
                               b_gate_x[l].reshape(n_gate_blocks, 1, GATE_BLOCK)], axis=-1)
        y_rec, q, k, v, fl = _inproj(x, row(g_mix[l]), sc1, sh1, w_in_b, w_f, row(g_q[l]), row(g_k[l]),
                                     conv_w[l], row(conv_b[l]), wax, bax, row(lru_logit[l]),
                                     d_rec=d_rec, d_att=d_att, n_heads=n_heads)

        f_rows = fl.transpose(0, 2, 1).reshape(bsz * n_heads, seq)
        b_rows = jnp.tile(b_forget[l], bsz).reshape(bsz * n_heads, 1)
        cum = _cum(f_rows, b_rows)
        y_att = _attn(q, k, v, cum)

        x1, h2 = _outproj(y_rec, y_att, x, w_out[l].astype(BF16), row(g_out_rec[l]), row(g_out_att[l]),
                          gt1, row(g_ffn[l]), sc2, sh2)
        a = _up(h2.reshape(t, d), w_up[l], d_ff)
        x = _down(a, w_down[l].astype(BF16), x1.reshape(t, d), gt2, seq=seq).reshape(bsz, seq, d)
    return x
```
